```python
import jax
import jax.numpy as jnp
from jax import lax
import numpy as np

D_MODEL = 1024
BATCH = 8
SEQ = 4096
DEPTH = 1
DEC_BATCH = 128
DEC_SEQ = 1
PAST_LEN = 8192
PAGE_SIZE = 128

A_HEADS = 8
A_HEAD_DIM = 64
A_WIDTH = A_HEADS * A_HEAD_DIM
MOBA_BLOCK = 256
MOBA_TOPK = 3
MOBA_QBLOCK = 32
ROPE_THETA = 500000.0
ROPE_DIM = A_HEAD_DIM // 4
R_HEADS = 4
R_KEY_DIM = 64
R_VAL_DIM = 128
R_QK_WIDTH = R_HEADS * R_KEY_DIM
R_V_WIDTH = R_HEADS * R_VAL_DIM
R_CHUNK = 128
R_ROPE_THETA = 10000.0
IN_SPLITS = (A_WIDTH, A_WIDTH, A_WIDTH, A_WIDTH, R_QK_WIDTH, R_QK_WIDTH, R_V_WIDTH, R_V_WIDTH, D_MODEL, D_MODEL)
IN_WIDTH = 4 * A_WIDTH + 2 * R_QK_WIDTH + 2 * R_V_WIDTH + 2 * D_MODEL
NORM_EPS = 1e-6
NEG = -1e30

kernel_name = 'hybrid_moba_retention_decode_step'


def rms_norm(x, g):
    xf = x.astype(jnp.float32)
    y = xf * lax.rsqrt(jnp.mean(xf * xf, axis=-1, keepdims=True) + NORM_EPS)
    return (y * g.astype(jnp.float32)).astype(x.dtype)


def rope(x, pos, rot_dim, theta):
    half = rot_dim // 2
    inv_freq = theta ** (-jnp.arange(half, dtype=jnp.float32) * (2.0 / rot_dim))
    ang = pos.astype(jnp.float32)[:, None] * inv_freq[None, :]
    cos = jnp.cos(ang)[None, :, None, :]
    sin = jnp.sin(ang)[None, :, None, :]
    xr = x[..., :rot_dim].astype(jnp.float32)
    x1, x2 = xr[..., :half], xr[..., half:]
    rot = jnp.concatenate([x1 * cos - x2 * sin, x2 * cos + x1 * sin], axis=-1).astype(x.dtype)
    return jnp.concatenate([rot, x[..., rot_dim:]], axis=-1)


def split_columns(z):
    offs = np.cumsum((0,) + IN_SPLITS)
    return [z[..., int(offs[i]):int(offs[i + 1])] for i in range(len(IN_SPLITS))]


def project_inputs(x, pos, g_pre, w_in, b_merge):
    b, s, _ = x.shape
    h = rms_norm(x, g_pre)
    z = jnp.einsum('bsd,de->bse', h, w_in)
    qa, ka, va, ga, qr, kr, vr, gr, ma, mr = split_columns(z)
    qa = rope(qa.reshape(b, s, A_HEADS, A_HEAD_DIM), pos, ROPE_DIM, ROPE_THETA)
    ka = rope(ka.reshape(b, s, A_HEADS, A_HEAD_DIM), pos, ROPE_DIM, ROPE_THETA)
    va = va.reshape(b, s, A_HEADS, A_HEAD_DIM)
    qr = rope(qr.reshape(b, s, R_HEADS, R_KEY_DIM), pos, R_KEY_DIM, R_ROPE_THETA)
    kr = rope(kr.reshape(b, s, R_HEADS, R_KEY_DIM), pos, R_KEY_DIM, R_ROPE_THETA) * (R_KEY_DIM ** -0.5)
    vr = vr.reshape(b, s, R_HEADS, R_VAL_DIM)
    ma = ma + b_merge[:D_MODEL]
    mr = mr + b_merge[D_MODEL:]
    return qa, ka, va, ga, qr, kr, vr, gr, ma, mr


def block_means(k_all):
    b, l, h, dh = k_all.shape
    n_full = l // MOBA_BLOCK
    kb = k_all[:, :n_full * MOBA_BLOCK].reshape(b, n_full, MOBA_BLOCK, h, dh)
    return jnp.mean(kb.astype(jnp.float32), axis=2)


def moba_attend(q, q_pos, k_all, v_all, k_means):
    b, h, nq, dh = q.shape
    seq_len = k_all.shape[1]
    n_full = k_means.shape[1]
    k_sel = min(MOBA_TOPK, n_full)
    own = q_pos // MOBA_BLOCK
    own_b = jnp.broadcast_to(own[None, None, :, None], (b, h, nq, 1))
    if k_sel > 0:
        blk = jnp.einsum('bhqd,bnhd->bhqn', q.astype(jnp.float32), k_means)
        fully_past = jnp.arange(n_full)[None, None, None, :] < own[None, None, :, None]
        blk = jnp.where(fully_past, blk, NEG)
        _, top_idx = lax.top_k(blk, k_sel)
        sel_ok = top_idx < own[None, None, :, None]
        blocks = jnp.concatenate([top_idx, own_b], axis=-1)
        blk_ok = jnp.concatenate([sel_ok, jnp.ones_like(own_b, dtype=bool)], axis=-1)
    else:
        blocks = own_b
        blk_ok = jnp.ones_like(own_b, dtype=bool)
    n_blk = blocks.shape[-1]
    tok = blocks[..., None] * MOBA_BLOCK + jnp.arange(MOBA_BLOCK)
    key_ok = blk_ok[..., None] & (tok <= q_pos[None, None, :, None, None])
    tok_flat = jnp.minimum(tok, seq_len - 1).reshape(b, h, -1)
    bi = jnp.arange(b)[:, None, None]
    hi = jnp.arange(h)[None, :, None]
    kg = k_all[bi, tok_flat, hi].reshape(b, h, nq, n_blk * MOBA_BLOCK, dh)
    vg = v_all[bi, tok_flat, hi].reshape(b, h, nq, n_blk * MOBA_BLOCK, dh)
    s = jnp.einsum('bhqd,bhqkd->bhqk', q, kg, preferred_element_type=jnp.float32) * (dh ** -0.5)
    s = jnp.where(key_ok.reshape(b, h, nq, -1), s, NEG)
    p = jax.nn.softmax(s, axis=-1)
    return jnp.einsum('bhqk,bhqkd->bhqd', p.astype(vg.dtype), vg)


def moba_prompt(qa, ka, va):
    b, s, h, dh = qa.shape
    k_means = block_means(ka)
    nq = s // MOBA_QBLOCK
    qb = qa.reshape(b, nq, MOBA_QBLOCK, h, dh).transpose(1, 0, 3, 2, 4)
    pb = jnp.arange(s, dtype=jnp.int32).reshape(nq, MOBA_QBLOCK)
    out = lax.map(lambda qp: moba_attend(qp[0], qp[1], ka, va, k_means), (qb, pb))
    return out.transpose(1, 0, 3, 2, 4).reshape(b, s, h * dh)


def retention_log_decay():
    return jnp.log(1.0 - jnp.exp2(-5.0 - jnp.arange(R_HEADS, dtype=jnp.float32)))


def retention_chunk(r, q, k, v):
    c = q.shape[2]
    lg = retention_log_decay()
    idx = jnp.arange(c, dtype=jnp.float32)
    diff = idx[:, None] - idx[None, :]
    dmat = jnp.where(diff >= 0, jnp.exp(lg[:, None, None] * jnp.maximum(diff, 0.0)), 0.0)
    qf, kf, vf = q.astype(jnp.float32), k.astype(jnp.float32), v.astype(jnp.float32)
    r = r.astype(jnp.float32)
    scores = jnp.einsum('bhid,bhjd->bhij', qf, kf) * dmat
    inner = jnp.einsum('bhij,bhje->bhie', scores, vf)
    q_dec = qf * jnp.exp(lg[:, None] * (idx + 1.0))[None, :, :, None]
    cross = jnp.einsum('bhid,bhde->bhie', q_dec, r)
    k_dec = kf * jnp.exp(lg[:, None] * (c - 1.0 - idx))[None, :, :, None]
    r_new = jnp.exp(lg * c)[None, :, None, None] * r + jnp.einsum('bhjd,bhje->bhde', k_dec, vf)
    return inner + cross, r_new


def retention_prompt(qr, kr, vr):
    b, h, s, _ = qr.shape
    nc = s // R_CHUNK

    def to_chunks(t):
        return t.reshape(b, h, nc, R_CHUNK, t.shape[-1]).transpose(2, 0, 1, 3, 4)

    def step(r, xs):
        o, r_new = retention_chunk(r, xs[0], xs[1], xs[2])
        return r_new, o

    r0 = jnp.zeros((b, h, R_KEY_DIM, R_VAL_DIM), jnp.float32)
    r_fin, o = lax.scan(step, r0, (to_chunks(qr), to_chunks(kr), to_chunks(vr)))
    return o.transpose(1, 2, 0, 3, 4).reshape(b, h, s, R_VAL_DIM), r_fin


def retention_head_norm(o, dtype):
    o = o * lax.rsqrt(jnp.mean(o * o, axis=-1, keepdims=True) + NORM_EPS)
    b, h, s, dv = o.shape
    return o.transpose(0, 2, 1, 3).reshape(b, s, h * dv).astype(dtype)


def merge_outputs(x, ya, yr, ga, gr, ma, mr, w_pa, w_pr, w_out, g_post):
    pa = jnp.einsum('bse,ed->bsd', ya * jax.nn.silu(ga), w_pa)
    pr = jnp.einsum('bse,ed->bsd', yr * jax.nn.silu(gr), w_pr)
    m = jax.nn.sigmoid(ma) * pa + jax.nn.sigmoid(mr) * pr
    o = jnp.einsum('bsd,de->bse', m, w_out)
    return x + rms_norm(o, g_post)


def decoder_layer(x_p, x_s, cache_k, cache_v, state_r, page_table, g_pre, w_in, b_merge, w_pa, w_pr, w_out, g_post):
    bs, ss, _ = x_s.shape
    pos_p = jnp.arange(x_p.shape[1], dtype=jnp.int32)
    pos_s = PAST_LEN + jnp.arange(ss, dtype=jnp.int32)
    qa, ka, va, ga, qr, kr, vr, gr, ma, mr = project_inputs(x_p, pos_p, g_pre, w_in, b_merge)
    ya = moba_prompt(qa, ka, va)
    yr_raw, r_p = retention_prompt(qr.transpose(0, 2, 1, 3), kr.transpose(0, 2, 1, 3), vr.transpose(0, 2, 1, 3))
    yr = retention_head_norm(yr_raw, x_p.dtype)
    y_p = merge_outputs(x_p, ya, yr, ga, gr, ma, mr, w_pa, w_pr, w_out, g_post)
    qs, ks, vs, gas, qrs, krs, vrs, grs, mas, mrs = project_inputs(x_s, pos_s, g_pre, w_in, b_merge)
    past_k = cache_k[page_table].reshape(bs, -1, A_HEADS, A_HEAD_DIM)
    past_v = cache_v[page_table].reshape(bs, -1, A_HEADS, A_HEAD_DIM)
    k_all = jnp.concatenate([past_k.astype(ks.dtype), ks], axis=1)
    v_all = jnp.concatenate([past_v.astype(vs.dtype), vs], axis=1)
    ya_s = moba_attend(qs.transpose(0, 2, 1, 3), pos_s, k_all, v_all, block_means(k_all))
    ya_s = ya_s.transpose(0, 2, 1, 3).reshape(bs, ss, A_WIDTH)
    yr_s_raw, r_s = retention_chunk(state_r, qrs.transpose(0, 2, 1, 3), krs.transpose(0, 2, 1, 3), vrs.transpose(0, 2, 1, 3))
    yr_s = retention_head_norm(yr_s_raw, x_s.dtype)
    y_s = merge_outputs(x_s, ya_s, yr_s, gas, grs, mas, mrs, w_pa, w_pr, w_out, g_post)
    return y_p, y_s, ka, va, r_p, ks, vs, r_s


def setup_inputs(seed: int = 0) -> dict:
    key = jax.random.key(seed)
    ks = jax.random.split(key, 13)
    f = jnp.float32
    n_pages = PAST_LEN // PAGE_SIZE
    n_pool = (DEC_BATCH * n_pages * 5) // 4
    x_prompt = jax.random.normal(ks[0], (BATCH, SEQ, D_MODEL), f)
    x_sample = jax.random.normal(ks[1], (DEC_BATCH, DEC_SEQ, D_MODEL), f)
    cache_k = jax.random.normal(ks[2], (DEPTH, n_pool, PAGE_SIZE, A_HEADS, A_HEAD_DIM), f)
    cache_v = jax.random.normal(ks[3], (DEPTH, n_pool, PAGE_SIZE, A_HEADS, A_HEAD_DIM), f)
    state_ret = 0.1 * jax.random.normal(ks[4], (DEPTH, DEC_BATCH, R_HEADS, R_KEY_DIM, R_VAL_DIM), f)
    page_table = jax.random.permutation(ks[5], n_pool)[:DEC_BATCH * n_pages].reshape(DEC_BATCH, n_pages).astype(jnp.int32)
    norm_pre_g = 1.0 + 0.1 * jax.random.normal(ks[6], (DEPTH, D_MODEL), f)
    w_in = jax.random.normal(ks[7], (DEPTH, D_MODEL, IN_WIDTH), f) * (D_MODEL ** -0.5)
    b_merge = 0.1 * jax.random.normal(ks[8], (DEPTH, 2 * D_MODEL), f)
    w_proj_a = jax.random.normal(ks[9], (DEPTH, A_WIDTH, D_MODEL), f) * (A_WIDTH ** -0.5)
    w_proj_r = jax.random.normal(ks[10], (DEPTH, R_V_WIDTH, D_MODEL), f) * (R_V_WIDTH ** -0.5)
    w_out = jax.random.normal(ks[11], (DEPTH, D_MODEL, D_MODEL), f) * (D_MODEL ** -0.5)
    norm_post_g = 1.0 + 0.1 * jax.random.normal(ks[12], (DEPTH, D_MODEL), f)
    return {'x_prompt': x_prompt, 'x_sample': x_sample, 'cache_k': cache_k, 'cache_v': cache_v,
            'state_ret': state_ret, 'page_table': page_table, 'norm_pre_g': norm_pre_g, 'w_in': w_in,
            'b_merge': b_merge, 'w_proj_a': w_proj_a, 'w_proj_r': w_proj_r, 'w_out': w_out,
            'norm_post_g': norm_post_g}


def reference(x_prompt, x_sample, cache_k, cache_v, state_ret, page_table, norm_pre_g, w_in, b_merge,
              w_proj_a, w_proj_r, w_out, norm_post_g):
    h_p, h_s = x_prompt, x_sample
    kp, vp, rp, ksm, vsm, rsm = [], [], [], [], [], []
    for layer in range(DEPTH):
        h_p, h_s, k_p, v_p, r_p, k_s, v_s, r_s = decoder_layer(
            h_p, h_s, cache_k[layer], cache_v[layer], state_ret[layer], page_table,
            norm_pre_g[layer], w_in[layer], b_merge[layer], w_proj_a[layer], w_proj_r[layer],
            w_out[layer], norm_post_g[layer])
        kp.append(k_p)
        vp.append(v_p)
        rp.append(r_p)
        ksm.append(k_s)
        vsm.append(v_s)
        rsm.append(r_s)
    return (h_p, h_s, jnp.stack(kp), jnp.stack(vp), jnp.stack(rp), jnp.stack(ksm), jnp.stack(vsm), jnp.stack(rsm))
```

```python
import functools

import numpy as np
import jax
import jax.numpy as jnp
from jax import lax
from jax.experimental import pallas as pl
from jax.experimental.pallas import tpu as pltpu

F32 = jnp.float32
BF16 = jnp.bfloat16
I32 = jnp.int32

D_MODEL = 1024
A_HEADS = 8
A_HEAD_DIM = 64
A_WIDTH = A_HEADS * A_HEAD_DIM
MOBA_BLOCK = 256
MOBA_TOPK = 3
ROPE_THETA = 500000.0
ROPE_DIM = A_HEAD_DIM // 4
R_HEADS = 4
R_KEY_DIM = 64
R_VAL_DIM = 128
R_QK_WIDTH = R_HEADS * R_KEY_DIM
R_V_WIDTH = R_HEADS * R_VAL_DIM
R_ROPE_THETA = 10000.0
PAGE_SIZE = 128
NORM_EPS = 1e-6
NEG = -1e30

_OFFS = np.cumsum((0, A_WIDTH, A_WIDTH, A_WIDTH, A_WIDTH, R_QK_WIDTH, R_QK_WIDTH, R_V_WIDTH, R_V_WIDTH, D_MODEL, D_MODEL))

LANES = 128
SUBLANES = 8
VMEM_LIMIT_BYTES = 56 * 1024 * 1024

RET_CHUNK = 256
PROMPT_TOKENS_PER_TILE = 512
PAGES_PER_STEP = 16
PAGES_PER_BLOCK = MOBA_BLOCK // PAGE_SIZE
SAMPLE_RET_BATCH = 16


def _params(*semantics):
    return pltpu.CompilerParams(dimension_semantics=semantics, vmem_limit_bytes=VMEM_LIMIT_BYTES)


def _rms_normed(x, g):
    return x * lax.rsqrt(jnp.mean(x * x, axis=-1, keepdims=True) + NORM_EPS) * g


def _rope_tables(pos, rot_dim, head_dim, theta):
    half = rot_dim // 2
    inv_freq = theta ** (-jnp.arange(half, dtype=F32) * (2.0 / rot_dim))
    ang = pos.astype(F32)[:, None] * inv_freq[None, :]
    cos, sin = jnp.cos(ang), jnp.sin(ang)
    p = pos.shape[0]
    rest = head_dim - rot_dim
    zh = jnp.zeros((p, half), F32)
    c = jnp.concatenate([cos, cos, jnp.ones((p, rest), F32)], axis=1)
    s1 = jnp.concatenate([-sin, zh, jnp.zeros((p, rest), F32)], axis=1)
    s2 = jnp.concatenate([zh, sin, jnp.zeros((p, rest), F32)], axis=1)
    reps = LANES // head_dim
    return tuple(jnp.tile(t, (1, reps)) for t in (c, s1, s2))


def _retention_log_decay():
    return jnp.log(1.0 - jnp.exp2(-5.0 - jnp.arange(R_HEADS, dtype=F32)))


def _rope_chunks(z, c_ref, s1_ref, s2_ref, half):
    c, s1, s2 = c_ref[...], s1_ref[...], s2_ref[...]
    out = []
    for j in range(z.shape[1] // LANES):
        zs = z[:, j * LANES:(j + 1) * LANES]
        up = pltpu.roll(zs, LANES - half, 1)
        down = pltpu.roll(zs, half, 1)
        out.append(zs * c + up * s1 + down * s2)
    return out


def _store_chunks(ref, chunks, scale=None):
    for j, ch in enumerate(chunks):
        if scale is not None:
            ch = ch * scale
        ref[:, j * LANES:(j + 1) * LANES] = ch.astype(ref.dtype)


def _inproj_kernel(x_ref, g_ref, w_ref, ca_ref, sa1_ref, sa2_ref, cr_ref, sr1_ref, sr2_ref,
                   q_ref, kf_ref, kb_ref, vf_ref, vb_ref, qr_ref, kr_ref, vr_ref, *km_refs):
    h = _rms_normed(x_ref[...], g_ref[...]).astype(BF16)

    def proj(lo, hi):
        return jnp.dot(h, w_ref[:, lo:hi], preferred_element_type=F32)

    a, rq, rv = A_WIDTH, R_QK_WIDTH, R_V_WIDTH
    _store_chunks(q_ref, _rope_chunks(proj(0, a), ca_ref, sa1_ref, sa2_ref, ROPE_DIM // 2))
    k_chunks = _rope_chunks(proj(a, 2 * a), ca_ref, sa1_ref, sa2_ref, ROPE_DIM // 2)
    _store_chunks(kf_ref, k_chunks)
    _store_chunks(kb_ref, k_chunks)
    if km_refs:
        (km_ref,) = km_refs
        for j, ch in enumerate(k_chunks):
            for blk in range(ch.shape[0] // MOBA_BLOCK):
                rows = ch[blk * MOBA_BLOCK:(blk + 1) * MOBA_BLOCK, :]
                km_ref[0, blk:blk + 1, j * LANES:(j + 1) * LANES] = (
                    jnp.sum(rows, axis=0, keepdims=True) * (1.0 / MOBA_BLOCK))
    v = proj(2 * a, 3 * a)
    vf_ref[...] = v
    vb_ref[...] = v.astype(vb_ref.dtype)
    o = 3 * a
    _store_chunks(qr_ref, _rope_chunks(proj(o, o + rq), cr_ref, sr1_ref, sr2_ref, R_KEY_DIM // 2))
    _store_chunks(kr_ref, _rope_chunks(proj(o + rq, o + 2 * rq), cr_ref, sr1_ref, sr2_ref, R_KEY_DIM // 2),
                  scale=R_KEY_DIM ** -0.5)
    vr_ref[...] = proj(o + 2 * rq, o + 2 * rq + rv).astype(vr_ref.dtype)


def _in_projection(x2d, g_pre, w_a, tabs_a, tabs_r, *, tm, with_means, act_dtype):
    t = x2d.shape[0]
    n = t // tm
    tab_blocks = tabs_a[0].shape[0] // tm
    tab_spec = pl.BlockSpec((tm, LANES), lambda i: (i % tab_blocks, 0))
    row = lambda width: pl.BlockSpec((tm, width), lambda i: (i, 0))
    const = lambda shape: pl.BlockSpec(shape, lambda i: (0,) * len(shape))
    out_shape = [
        jax.ShapeDtypeStruct((t, A_WIDTH), act_dtype),
        jax.ShapeDtypeStruct((t, A_WIDTH), F32),
        jax.ShapeDtypeStruct((t, A_WIDTH), BF16),
        jax.ShapeDtypeStruct((t, A_WIDTH), F32),
        jax.ShapeDtypeStruct((t, A_WIDTH), BF16),
        jax.ShapeDtypeStruct((t, R_QK_WIDTH), act_dtype),
        jax.ShapeDtypeStruct((t, R_QK_WIDTH), act_dtype),
        jax.ShapeDtypeStruct((t, R_V_WIDTH), act_dtype),
    ]
    out_specs = [row(A_WIDTH)] * 5 + [row(R_QK_WIDTH)] * 2 + [row(R_V_WIDTH)]
    if with_means:
        nb = tm // MOBA_BLOCK
        out_shape.append(jax.ShapeDtypeStruct((n, nb, A_WIDTH), F32))
        out_specs.append(pl.BlockSpec((1, nb, A_WIDTH), lambda i: (i, 0, 0)))
    return pl.pallas_call(
        _inproj_kernel,
        grid=(n,),
        in_specs=[row(D_MODEL), const((1, D_MODEL)), const(w_a.shape)] + [tab_spec] * 6,
        out_specs=out_specs,
        out_shape=out_shape,
        compiler_params=_params("arbitrary"),
        name="in_projection",
    )(x2d, g_pre, w_a, *tabs_a, *tabs_r)


def _moba_prompt_kernel(q_ref, k_ref, v_ref, km_ref, o_ref, vt_ref, qs_ref, bias_ref, acc_ref):
    i = pl.program_id(2)
    nb = MOBA_BLOCK
    hd = A_HEAD_DIM
    n_blocks = k_ref.shape[0] // nb

    @pl.when(i == 0)
    def _():
        for c in range(n_blocks):
            vt_ref[c] = v_ref[c * nb:(c + 1) * nb, :].astype(F32).T.astype(BF16)

    qt = q_ref[...].astype(F32).T
    row = lax.broadcasted_iota(I32, qt.shape, 0)
    km = km_ref[...]
    km_hi = km.astype(BF16)
    km_lo = (km - km_hi.astype(F32)).astype(BF16)
    jrow = lax.broadcasted_iota(I32, (n_blocks, nb), 0)
    for h in range(2):
        qh = jnp.where((row >= hd) == bool(h), qt, 0.0)
        qs_ref[h] = (qh * (hd ** -0.5)).astype(BF16)
        qb = qh.astype(BF16)
        sc = (jnp.dot(km_hi, qb, preferred_element_type=F32)
              + jnp.dot(km_lo, qb, preferred_element_type=F32))
        sc = jnp.where(jrow < i, sc, NEG)
        cnt = jnp.zeros(sc.shape, I32)
        for jp in range(n_blocks):
            r = sc[jp:jp + 1, :]
            beats = (r > sc) | ((r == sc) & (jp < jrow))
            cnt = cnt + beats.astype(I32)
        sel = (jrow < i) & (cnt < MOBA_TOPK)
        bias = jnp.where(sel, 0.0, NEG)
        for jp in range(n_blocks):
            bias_ref[jp, h:h + 1, :] = bias[jp:jp + 1, :]

    key_idx = lax.broadcasted_iota(I32, (nb, nb), 0)
    qry_idx = lax.broadcasted_iota(I32, (nb, nb), 1)
    k_own = k_ref[pl.ds(pl.multiple_of(i * nb, nb), nb), :]
    vt_own = vt_ref[i]
    stats = []
    for h in range(2):
        s = jnp.dot(k_own, qs_ref[h], preferred_element_type=F32)
        s = jnp.where(key_idx <= qry_idx, s, NEG)
        m = jnp.max(s, axis=0, keepdims=True)
        p = jnp.exp(s - m)
        l = jnp.sum(p, axis=0, keepdims=True)
        sl = slice(h * hd, (h + 1) * hd)
        acc_ref[sl, :] = jnp.dot(vt_own[sl, :], p.astype(BF16), preferred_element_type=F32)
        stats += [m, l]

    def body(j, carry):
        kj = k_ref[pl.ds(pl.multiple_of(j * nb, nb), nb), :]
        vtj = vt_ref[j]
        new = []
        for h in range(2):
            m, l = carry[2 * h], carry[2 * h + 1]
            s = jnp.dot(kj, qs_ref[h], preferred_element_type=F32) + bias_ref[j, h:h + 1, :]
            m_new = jnp.maximum(m, jnp.max(s, axis=0, keepdims=True))
            alpha = jnp.exp(m - m_new)
            p = jnp.exp(s - m_new)
            l_new = alpha * l + jnp.sum(p, axis=0, keepdims=True)
            sl = slice(h * hd, (h + 1) * hd)
            acc_ref[sl, :] = alpha * acc_ref[sl, :] + jnp.dot(vtj[sl, :], p.astype(BF16),
                                                               preferred_element_type=F32)
            new += [m_new, l_new]
        return tuple(new)

    m0, l0, m1, l1 = lax.fori_loop(0, i, body, tuple(stats))
    out_t = jnp.concatenate([acc_ref[0:hd, :] / l0, acc_ref[hd:2 * hd, :] / l1], axis=0)
    o_ref[...] = out_t.T.astype(o_ref.dtype)


def _moba_prompt(q, k, v, km, *, batch, seq):
    nb = MOBA_BLOCK
    n_blocks = seq // nb
    pairs = A_HEADS // 2
    pw = 2 * A_HEAD_DIM
    return pl.pallas_call(
        _moba_prompt_kernel,
        grid=(batch, pairs, n_blocks),
        in_specs=[
            pl.BlockSpec((nb, pw), lambda b, p, i: (b * n_blocks + i, p)),
            pl.BlockSpec((seq, pw), lambda b, p, i: (b, p)),
            pl.BlockSpec((seq, pw), lambda b, p, i: (b, p)),
            pl.BlockSpec((n_blocks, pw), lambda b, p, i: (b, p)),
        ],
        out_specs=pl.BlockSpec((nb, pw), lambda b, p, i: (b * n_blocks + i, p)),
        out_shape=jax.ShapeDtypeStruct((batch * seq, A_WIDTH), BF16),
        scratch_shapes=[
            pltpu.VMEM((n_blocks, pw, nb), BF16),
            pltpu.VMEM((2, pw, nb), BF16),
            pltpu.VMEM((n_blocks, SUBLANES, nb), F32),
            pltpu.VMEM((pw, nb), F32),
        ],
        compiler_params=_params("arbitrary", "arbitrary", "arbitrary"),
        name="moba_prompt",
    )(q, k, v, km)


def _retention_prompt_kernel(q_ref, k_ref, v_ref, dmat_ref, qdec_ref, kdec_ref, y_ref, rfin_ref, r_ref,
                             *, state_decay):
    c = pl.program_id(1)
    kd, vd = R_KEY_DIM, R_VAL_DIM

    @pl.when(c == 0)
    def _():
        r_ref[...] = jnp.zeros(r_ref.shape, F32)

    chunk = q_ref.shape[0]
    lane = lax.broadcasted_iota(I32, (chunk, 2 * kd), 1)
    for p in range(R_HEADS // 2):
        q2 = q_ref[:, p * 2 * kd:(p + 1) * 2 * kd].astype(F32)
        k2 = k_ref[:, p * 2 * kd:(p + 1) * 2 * kd]
        qd2 = q2 * qdec_ref[p]
        kdt = (k2.astype(F32) * kdec_ref[p]).T.astype(BF16)
        r2 = r_ref[p]
        r2b = r2.astype(BF16)
        for hl in range(2):
            h = 2 * p + hl
            own = (lane >= kd) == bool(hl)
            qm = jnp.where(own, q2, 0.0).astype(BF16)
            qdm = jnp.where(own, qd2, 0.0).astype(BF16)
            sc = lax.dot_general(qm, k2, (((1,), (1,)), ((), ())), preferred_element_type=F32) * dmat_ref[h]
            vh = v_ref[:, h * vd:(h + 1) * vd]
            o = (jnp.dot(sc.astype(BF16), vh, preferred_element_type=F32)
                 + jnp.dot(qdm, r2b, preferred_element_type=F32))
            sl = slice(hl * kd, (hl + 1) * kd)
            u = jnp.dot(kdt[sl, :], vh, preferred_element_type=F32)
            r_ref[p, sl, :] = state_decay[h] * r2[sl, :] + u
            y = o * lax.rsqrt(jnp.mean(o * o, axis=-1, keepdims=True) + NORM_EPS)
            y_ref[:, h * vd:(h + 1) * vd] = y.astype(y_ref.dtype)

    @pl.when(c == pl.num_programs(1) - 1)
    def _():
        for h in range(R_HEADS):
            rfin_ref[0, h] = r_ref[h // 2, (h % 2) * kd:(h % 2 + 1) * kd, :]


def _retention_prompt(qr, kr, vr, *, batch, seq):
    chunk = RET_CHUNK
    nc = seq // chunk
    lg = _retention_log_decay()
    idx = jnp.arange(chunk, dtype=F32)
    diff = idx[:, None] - idx[None, :]
    dmat = jnp.where(diff >= 0, jnp.exp(lg[:, None, None] * jnp.maximum(diff, 0.0)), 0.0)
    qdec = jnp.exp(lg[:, None] * (idx + 1.0))
    kdec = jnp.exp(lg[:, None] * (chunk - 1.0 - idx))
    pair_table = lambda t: jnp.repeat(t.reshape(R_HEADS // 2, 2, chunk), R_KEY_DIM, axis=1).transpose(0, 2, 1)
    lg32 = np.log(np.float32(1.0) - np.exp2(np.float32(-5.0) - np.arange(R_HEADS, dtype=np.float32)))
    state_decay = tuple(float(np.exp(lg32[h] * np.float32(chunk))) for h in range(R_HEADS))
    const = lambda shape: pl.BlockSpec(shape, lambda b, c: (0,) * len(shape))
    return pl.pallas_call(
        functools.partial(_retention_prompt_kernel, state_decay=state_decay),
        grid=(batch, nc),
        in_specs=[
            pl.BlockSpec((chunk, R_QK_WIDTH), lambda b, c: (b * nc + c, 0)),
            pl.BlockSpec((chunk, R_QK_WIDTH), lambda b, c: (b * nc + c, 0)),
            pl.BlockSpec((chunk, R_V_WIDTH), lambda b, c: (b * nc + c, 0)),
            const((R_HEADS, chunk, chunk)),
            const((R_HEADS // 2, chunk, 2 * R_KEY_DIM)),
            const((R_HEADS // 2, chunk, 2 * R_KEY_DIM)),
        ],
        out_specs=[
            pl.BlockSpec((chunk, R_V_WIDTH), lambda b, c: (b * nc + c, 0)),
            pl.BlockSpec((1, R_HEADS, R_KEY_DIM, R_VAL_DIM), lambda b, c: (b, 0, 0, 0)),
        ],
        out_shape=[
            jax.ShapeDtypeStruct((batch * seq, R_V_WIDTH), BF16),
            jax.ShapeDtypeStruct((batch, R_HEADS, R_KEY_DIM, R_VAL_DIM), F32),
        ],
        scratch_shapes=[pltpu.VMEM((R_HEADS // 2, 2 * R_KEY_DIM, R_VAL_DIM), F32)],
        compiler_params=_params("arbitrary", "arbitrary"),
        name="retention_prompt",
    )(qr, kr, vr, dmat, pair_table(qdec), pair_table(kdec))


def _merge_kernel(x_ref, ya_ref, yr_ref, g_ref, wg_ref, wm_ref, b_ref, wpa_ref, wpr_ref, wo_ref, gp_ref, y_ref):
    x = x_ref[...]
    h = _rms_normed(x, g_ref[...]).astype(BF16)
    gates = jnp.dot(h, wg_ref[...], preferred_element_type=F32)
    ua = (ya_ref[...].astype(F32) * jax.nn.silu(gates[:, :A_WIDTH])).astype(BF16)
    ur = (yr_ref[...].astype(F32) * jax.nn.silu(gates[:, A_WIDTH:])).astype(BF16)
    pa = jnp.dot(ua, wpa_ref[...], preferred_element_type=F32)
    pr = jnp.dot(ur, wpr_ref[...], preferred_element_type=F32)
    mix = jnp.dot(h, wm_ref[...], preferred_element_type=F32) + b_ref[...]
    m = jax.nn.sigmoid(mix[:, :D_MODEL]) * pa + jax.nn.sigmoid(mix[:, D_MODEL:]) * pr
    o = jnp.dot(m.astype(BF16), wo_ref[...], preferred_element_type=F32)
    y_ref[...] = x + _rms_normed(o, gp_ref[...])


def _merge(x2d, ya, yr, g_pre, w_g, w_m, b_merge, w_pa, w_pr, w_out, g_post, *, tm):
    t = x2d.shape[0]
    row = lambda width: pl.BlockSpec((tm, width), lambda i: (i, 0))
    const = lambda a: pl.BlockSpec(a.shape, lambda i: (0,) * a.ndim)
    return pl.pallas_call(
        _merge_kernel,
        grid=(t // tm,),
        in_specs=[row(D_MODEL), row(A_WIDTH), row(R_V_WIDTH), const(g_pre), const(w_g), const(w_m),
                  const(b_merge), const(w_pa), const(w_pr), const(w_out), const(g_post)],
        out_specs=row(D_MODEL),
        out_shape=jax.ShapeDtypeStruct((t, D_MODEL), F32),
        compiler_params=_params("arbitrary"),
        name="gate_merge",
    )(x2d, ya, yr, g_pre, w_g, w_m, b_merge, w_pa, w_pr, w_out, g_post)


def _split_bf16(x):
    hi = x.astype(BF16)
    return hi, (x - hi.astype(F32)).astype(BF16)


def _select_blocks_kernel(pt_ref, q_ref, *refs, n_blocks):
    del pt_ref
    pages, idx_ref, km_ref = refs[:-2], refs[-2], refs[-1]
    c = pl.program_id(1)
    blocks_per_step = len(pages) // PAGES_PER_BLOCK
    lane = lax.broadcasted_iota(I32, km_ref.shape, 1)

    @pl.when(c == 0)
    def _():
        km_ref[...] = jnp.zeros(km_ref.shape, F32)

    for r in range(blocks_per_step):
        tot = pages[r * PAGES_PER_BLOCK][0]
        for half in range(1, PAGES_PER_BLOCK):
            tot = tot + pages[r * PAGES_PER_BLOCK + half][0]
        col = jnp.sum(tot.reshape(A_WIDTH, PAGE_SIZE), axis=1, keepdims=True) * (1.0 / MOBA_BLOCK)
        km_ref[...] = jnp.where(lane == c * blocks_per_step + r, col, km_ref[...])

    @pl.when(c == pl.num_programs(1) - 1)
    def _():
        q = jnp.broadcast_to(q_ref[0], (A_HEADS, A_WIDTH))
        head = lax.broadcasted_iota(I32, q.shape, 0)
        col_head = lax.broadcasted_iota(I32, q.shape, 1) // A_HEAD_DIM
        q_hi, q_lo = _split_bf16(jnp.where(head == col_head, q, 0.0))
        km_hi, km_lo = _split_bf16(km_ref[...])
        dot = lambda a, b: jnp.dot(a, b, preferred_element_type=F32)
        sc = dot(q_hi, km_hi) + dot(q_hi, km_lo) + dot(q_lo, km_hi)
        blk = lax.broadcasted_iota(I32, sc.shape, 1)
        sc = jnp.where(blk < n_blocks, sc, -jnp.inf)
        out = jnp.zeros(sc.shape, I32)
        for t in range(MOBA_TOPK):
            mx = jnp.max(sc, axis=1, keepdims=True)
            first = jnp.min(jnp.where(sc == mx, blk, LANES), axis=1, keepdims=True)
            out = jnp.where(blk == t, first, out)
            sc = jnp.where(blk == first, -jnp.inf, sc)
        idx_ref[0] = out


def _select_blocks(cache_kt, pt_flat, q_s, *, batch, n_pages):
    g = PAGES_PER_STEP
    steps = n_pages // g
    n_blocks = n_pages // PAGES_PER_BLOCK

    def page_spec(r):
        return pl.BlockSpec((1, A_HEADS, A_HEAD_DIM, PAGE_SIZE),
                            lambda b, c, pt: (pt[b * n_pages + c * g + r], 0, 0, 0))

    return pl.pallas_call(
        functools.partial(_select_blocks_kernel, n_blocks=n_blocks),
        grid_spec=pltpu.PrefetchScalarGridSpec(
            num_scalar_prefetch=1,
            grid=(batch, steps),
            in_specs=[pl.BlockSpec((1, 1, A_WIDTH), lambda b, c, pt: (b, 0, 0))] + [page_spec(r) for r in range(g)],
            out_specs=pl.BlockSpec((1, A_HEADS, LANES), lambda b, c, pt: (b, 0, 0)),
            scratch_shapes=[pltpu.VMEM((A_WIDTH, LANES), F32)],
        ),
        out_shape=jax.ShapeDtypeStruct((batch, A_HEADS, LANES), I32),
        compiler_params=_params("arbitrary", "arbitrary"),
        name="select_blocks",
    )(pt_flat, q_s.reshape(batch, 1, A_WIDTH), *([cache_kt] * g))


def _sample_attn_kernel(sel_ref, pt_ref, q_ref, ks_ref, vs_ref, *refs):
    del sel_ref, pt_ref
    n_slabs = MOBA_TOPK * PAGES_PER_BLOCK
    o_ref = refs[-1]
    hd = A_HEAD_DIM
    nt = (((1,), (1,)), ((), ()))
    outs = []
    for hl in range(2):
        k_slabs = refs[hl * n_slabs:(hl + 1) * n_slabs]
        v_slabs = refs[(2 + hl) * n_slabs:(3 + hl) * n_slabs]
        cols = slice(hl * hd, (hl + 1) * hd)
        q = q_ref[0][:, cols] * (hd ** -0.5)
        q8 = jnp.broadcast_to(q, (SUBLANES, hd)).astype(BF16)
        s_self = jnp.sum(q * ks_ref[0][:, cols], axis=-1, keepdims=True)
        s = [jnp.dot(q8, kp[0, 0].astype(BF16), preferred_element_type=F32) for kp in k_slabs]
        m = s_self
        for sp in s:
            m = jnp.maximum(m, jnp.max(sp, axis=-1, keepdims=True))
        p_self = jnp.exp(s_self - m)
        l = p_self
        acc = p_self * vs_ref[0][:, cols]
        for sp, vp in zip(s, v_slabs):
            p = jnp.exp(sp - m)
            l = l + jnp.sum(p, axis=-1, keepdims=True)
            acc = acc + lax.dot_general(p.astype(BF16), vp[0, 0].astype(BF16), nt, preferred_element_type=F32)
        outs.append((acc / l)[0:1, :])
    o_ref[0] = jnp.concatenate(outs, axis=1)


def _sample_attention(sel_flat, pt_flat, q_s, k_s, v_s, cache_kt, cache_vt, *, batch, n_pages):
    pairs = A_HEADS // 2
    pw = 2 * A_HEAD_DIM

    def slab_spec(hl, t, half):
        def index_map(b, p, sel, pt):
            head = 2 * p + hl
            blk = sel[(b * A_HEADS + head) * MOBA_TOPK + t]
            return (pt[b * n_pages + blk * PAGES_PER_BLOCK + half], head, 0, 0)
        return pl.BlockSpec((1, 1, A_HEAD_DIM, PAGE_SIZE), index_map)

    slab_specs = [slab_spec(hl, t, half) for hl in range(2) for t in range(MOBA_TOPK)
                  for half in range(PAGES_PER_BLOCK)]
    tok_spec = pl.BlockSpec((1, 1, pw), lambda b, p, sel, pt: (b, 0, p))
    n_ops = len(slab_specs)
    return pl.pallas_call(
        _sample_attn_kernel,
        grid_spec=pltpu.PrefetchScalarGridSpec(
            num_scalar_prefetch=2,
            grid=(batch, pairs),
            in_specs=[tok_spec] * 3 + slab_specs + slab_specs,
            out_specs=tok_spec,
        ),
        out_shape=jax.ShapeDtypeStruct((batch, 1, A_WIDTH), F32),
        compiler_params=_params("arbitrary", "arbitrary"),
        name="sample_attention",
    )(sel_flat, pt_flat, q_s.reshape(batch, 1, A_WIDTH), k_s.reshape(batch, 1, A_WIDTH),
      v_s.reshape(batch, 1, A_WIDTH), *([cache_kt] * n_ops), *([cache_vt] * n_ops))


def _sample_retention_kernel(q_ref, k_ref, v_ref, r_ref, y_ref, rnew_ref, *, decay):
    bb = q_ref.shape[0]
    kd, vd = R_KEY_DIM, R_VAL_DIM
    reps = LANES // bb
    qt = jnp.concatenate([q_ref[...]] * reps, axis=0).T
    kt = jnp.concatenate([k_ref[...]] * reps, axis=0).T
    for bi in range(bb):
        for h in range(R_HEADS):
            rows = slice(h * kd, (h + 1) * kd)
            qc = qt[rows, bi:bi + 1]
            kc = kt[rows, bi:bi + 1]
            vrow = v_ref[bi:bi + 1, h * vd:(h + 1) * vd]
            r = r_ref[bi, rows, :]
            qk = jnp.sum(qc * kc, axis=0, keepdims=True)
            cross = jnp.sum((qc * decay[h]) * r, axis=0, keepdims=True)
            o = qk * vrow + cross
            rnew_ref[bi, rows, :] = decay[h] * r + kc * vrow
            y = o * lax.rsqrt(jnp.mean(o * o, axis=-1, keepdims=True) + NORM_EPS)
            y_ref[bi:bi + 1, h * vd:(h + 1) * vd] = y


def _sample_retention(qr_s, kr_s, vr_s, state):
    batch = qr_s.shape[0]
    bb = SAMPLE_RET_BATCH
    lg32 = np.log(np.float32(1.0) - np.exp2(np.float32(-5.0) - np.arange(R_HEADS, dtype=np.float32)))
    decay = tuple(float(np.exp(lg32[h])) for h in range(R_HEADS))
    state3 = state.reshape(batch, R_HEADS * R_KEY_DIM, R_VAL_DIM)
    y, r_new = pl.pallas_call(
        functools.partial(_sample_retention_kernel, decay=decay),
        grid=(batch // bb,),
        in_specs=[
            pl.BlockSpec((bb, R_QK_WIDTH), lambda i: (i, 0)),
            pl.BlockSpec((bb, R_QK_WIDTH), lambda i: (i, 0)),
            pl.BlockSpec((bb, R_V_WIDTH), lambda i: (i, 0)),
            pl.BlockSpec((bb, R_HEADS * R_KEY_DIM, R_VAL_DIM), lambda i: (i, 0, 0)),
        ],
        out_specs=[
            pl.BlockSpec((bb, R_V_WIDTH), lambda i: (i, 0)),
            pl.BlockSpec((bb, R_HEADS * R_KEY_DIM, R_VAL_DIM), lambda i: (i, 0, 0)),
        ],
        out_shape=[
            jax.ShapeDtypeStruct((batch, R_V_WIDTH), F32),
            jax.ShapeDtypeStruct(state3.shape, F32),
        ],
        compiler_params=_params("arbitrary"),
        name="sample_retention",
    )(qr_s, kr_s, vr_s, state3)
    return y, r_new.reshape(state.shape)


def _layer(x_p, x_s, cache_k, cache_v, state_r, page_table, g_pre, w_in, b_merge, w_pa, w_pr, w_out, g_post):
    batch, seq, _ = x_p.shape
    dec_batch, dec_seq, _ = x_s.shape
    assert dec_seq == 1
    n_pages = page_table.shape[1]
    past_len = n_pages * PAGE_SIZE
    assert seq % PROMPT_TOKENS_PER_TILE == 0 and seq % RET_CHUNK == 0
    assert n_pages % PAGES_PER_STEP == 0 and MOBA_TOPK <= n_pages // PAGES_PER_BLOCK <= LANES
    assert dec_batch % SAMPLE_RET_BATCH == 0

    o = _OFFS
    w_bf = w_in.astype(BF16)
    w_a = jnp.concatenate([w_bf[:, o[0]:o[3]], w_bf[:, o[4]:o[7]]], axis=1)
    w_g = jnp.concatenate([w_bf[:, o[3]:o[4]], w_bf[:, o[7]:o[8]]], axis=1)
    w_m = w_bf[:, o[8]:o[10]]
    w_pa_b, w_pr_b, w_out_b = w_pa.astype(BF16), w_pr.astype(BF16), w_out.astype(BF16)
    g_pre2, g_post2, b2 = g_pre.reshape(1, D_MODEL), g_post.reshape(1, D_MODEL), b_merge.reshape(1, 2 * D_MODEL)

    pos_p = jnp.arange(seq, dtype=I32)
    xp2 = x_p.reshape(batch * seq, D_MODEL)
    q, kf, kb, vf, vb, qr, kr, vr, km = _in_projection(
        xp2, g_pre2, w_a,
        _rope_tables(pos_p, ROPE_DIM, A_HEAD_DIM, ROPE_THETA),
        _rope_tables(pos_p, R_KEY_DIM, R_KEY_DIM, R_ROPE_THETA),
        tm=PROMPT_TOKENS_PER_TILE, with_means=True, act_dtype=BF16)
    km = km.reshape(batch * (seq // MOBA_BLOCK), A_WIDTH)
    ya = _moba_prompt(q, kb, vb, km, batch=batch, seq=seq)
    yr, r_p = _retention_prompt(qr, kr, vr, batch=batch, seq=seq)
    y_p = _merge(xp2, ya, yr, g_pre2, w_g, w_m, b2, w_pa_b, w_pr_b, w_out_b, g_post2, tm=PROMPT_TOKENS_PER_TILE)

    pos_s = jnp.full((dec_batch,), past_len, dtype=I32)
    xs2 = x_s.reshape(dec_batch, D_MODEL)
    q_s, k_s, _, v_s, _, qr_s, kr_s, vr_s = _in_projection(
        xs2, g_pre2, w_a,
        _rope_tables(pos_s, ROPE_DIM, A_HEAD_DIM, ROPE_THETA),
        _rope_tables(pos_s, R_KEY_DIM, R_KEY_DIM, R_ROPE_THETA),
        tm=dec_batch, with_means=False, act_dtype=F32)
    cache_kt = cache_k.transpose(0, 2, 3, 1)
    cache_vt = cache_v.transpose(0, 2, 3, 1)
    pt_flat = page_table.reshape(-1)
    sel = _select_blocks(cache_kt, pt_flat, q_s, batch=dec_batch, n_pages=n_pages)
    sel_flat = sel[:, :, :MOBA_TOPK].reshape(-1)
    ya_s = _sample_attention(sel_flat, pt_flat, q_s, k_s, v_s, cache_kt, cache_vt,
                             batch=dec_batch, n_pages=n_pages).reshape(dec_batch, A_WIDTH)
    yr_s, r_s = _sample_retention(qr_s, kr_s, vr_s, state_r)
    y_s = _merge(xs2, ya_s, yr_s, g_pre2, w_g, w_m, b2, w_pa_b, w_pr_b, w_out_b, g_post2, tm=dec_batch)

    return (y_p.reshape(x_p.shape), y_s.reshape(x_s.shape),
            kf.reshape(batch, seq, A_HEADS, A_HEAD_DIM), vf.reshape(batch, seq, A_HEADS, A_HEAD_DIM), r_p,
            k_s.reshape(dec_batch, 1, A_HEADS, A_HEAD_DIM), v_s.reshape(dec_batch, 1, A_HEADS, A_HEAD_DIM), r_s)


def kernel(x_prompt, x_sample, cache_k, cache_v, state_ret, page_table, norm_pre_g, w_in, b_merge, w_proj_a,
           w_proj_r, w_out, norm_post_g):
    h_p, h_s = x_prompt, x_sample
    outs = [[] for _ in range(6)]
    for layer in range(w_in.shape[0]):
        res = _layer(h_p, h_s, cache_k[layer], cache_v[layer], state_ret[layer], page_table,
                     norm_pre_g[layer], w_in[layer], b_merge[layer], w_proj_a[layer], w_proj_r[layer],
                     w_out[layer], norm_post_g[layer])
        h_p, h_s = res[0], res[1]
        for acc, leaf in zip(outs, res[2:]):
            acc.append(leaf)
    return (h_p, h_s) + tuple(jnp.stack(o) for o in outs)
```

```python
import functools

import numpy as np
import jax
import jax.numpy as jnp
from jax import lax
from jax.experimental import pallas as pl
from jax.experimental.pallas import tpu as pltpu

F32 = jnp.float32
BF16 = jnp.bfloat16
I32 = jnp.int32

D_MODEL = 1024
A_HEADS = 8
A_HEAD_DIM = 64
A_WIDTH = A_HEADS * A_HEAD_DIM
MOBA_BLOCK = 256
MOBA_TOPK = 3
ROPE_THETA = 500000.0
ROPE_DIM = A_HEAD_DIM // 4
R_HEADS = 4
R_KEY_DIM = 64
R_VAL_DIM = 128
R_QK_WIDTH = R_HEADS * R_KEY_DIM
R_V_WIDTH = R_HEADS * R_VAL_DIM
R_ROPE_THETA = 10000.0
PAGE_SIZE = 128
NORM_EPS = 1e-6
NEG = -1e30

_OFFS = np.cumsum((0, A_WIDTH, A_WIDTH, A_WIDTH, A_WIDTH, R_QK_WIDTH, R_QK_WIDTH, R_V_WIDTH, R_V_WIDTH, D_MODEL, D_MODEL))

LANES = 128
SUBLANES = 8
VMEM_LIMIT_BYTES = 56 * 1024 * 1024

MOBA_Q_SCALE = A_HEAD_DIM ** -0.5 * float(np.log2(np.e))
MOBA_GROUP = 4
BF16_SUBLANES = 16
MOBA_V_ROWS = A_HEAD_DIM + BF16_SUBLANES
RET_CHUNK = 256
PROMPT_TOKENS_PER_TILE = 512
PAGES_PER_STEP = 32
PAGES_PER_BLOCK = MOBA_BLOCK // PAGE_SIZE
SAMPLE_RET_BATCH = 16


def _params(*semantics):
    return pltpu.CompilerParams(dimension_semantics=semantics, vmem_limit_bytes=VMEM_LIMIT_BYTES)


def _rms_normed(x, g):
    return x * lax.rsqrt(jnp.mean(x * x, axis=-1, keepdims=True) + NORM_EPS) * g


def _rope_tables(pos, rot_dim, head_dim, theta):
    half = rot_dim // 2
    inv_freq = theta ** (-jnp.arange(half, dtype=F32) * (2.0 / rot_dim))
    ang = pos.astype(F32)[:, None] * inv_freq[None, :]
    cos, sin = jnp.cos(ang), jnp.sin(ang)
    p = pos.shape[0]
    rest = head_dim - rot_dim
    zh = jnp.zeros((p, half), F32)
    c = jnp.concatenate([cos, cos, jnp.ones((p, rest), F32)], axis=1)
    s1 = jnp.concatenate([-sin, zh, jnp.zeros((p, rest), F32)], axis=1)
    s2 = jnp.concatenate([zh, sin, jnp.zeros((p, rest), F32)], axis=1)
    reps = LANES // head_dim
    return tuple(jnp.tile(t, (1, reps)) for t in (c, s1, s2))


def _retention_log_decay():
    return jnp.log(1.0 - jnp.exp2(-5.0 - jnp.arange(R_HEADS, dtype=F32)))


def _rope_chunks(z, c_ref, s1_ref, s2_ref, half):
    c, s1, s2 = c_ref[...], s1_ref[...], s2_ref[...]
    out = []
    for j in range(z.shape[1] // LANES):
        zs = z[:, j * LANES:(j + 1) * LANES]
        up = pltpu.roll(zs, LANES - half, 1)
        down = pltpu.roll(zs, half, 1)
        out.append(zs * c + up * s1 + down * s2)
    return out


def _store_chunks(ref, chunks, scale=None):
    for j, ch in enumerate(chunks):
        if scale is not None:
            ch = ch * scale
        ref[:, j * LANES:(j + 1) * LANES] = ch.astype(ref.dtype)


def _inproj_kernel(x_ref, g_ref, w_ref, ca_ref, sa1_ref, sa2_ref, cr_ref, sr1_ref, sr2_ref,
                   q_ref, kf_ref, kb_ref, vf_ref, vb_ref, qr_ref, kr_ref, vr_ref, *km_refs, q_scale):
    h = _rms_normed(x_ref[...], g_ref[...]).astype(BF16)

    def proj(lo, hi):
        return jnp.dot(h, w_ref[:, lo:hi], preferred_element_type=F32)

    a, rq, rv = A_WIDTH, R_QK_WIDTH, R_V_WIDTH
    _store_chunks(q_ref, _rope_chunks(proj(0, a), ca_ref, sa1_ref, sa2_ref, ROPE_DIM // 2), scale=q_scale)
    k_chunks = _rope_chunks(proj(a, 2 * a), ca_ref, sa1_ref, sa2_ref, ROPE_DIM // 2)
    _store_chunks(kf_ref, k_chunks)
    _store_chunks(kb_ref, k_chunks)
    if km_refs:
        (km_ref,) = km_refs
        for j, ch in enumerate(k_chunks):
            for blk in range(ch.shape[0] // MOBA_BLOCK):
                rows = ch[blk * MOBA_BLOCK:(blk + 1) * MOBA_BLOCK, :]
                km_ref[0, blk:blk + 1, j * LANES:(j + 1) * LANES] = (
                    jnp.sum(rows, axis=0, keepdims=True) * (1.0 / MOBA_BLOCK))
    v = proj(2 * a, 3 * a)
    vf_ref[...] = v
    vb_ref[...] = v.astype(vb_ref.dtype)
    o = 3 * a
    _store_chunks(qr_ref, _rope_chunks(proj(o, o + rq), cr_ref, sr1_ref, sr2_ref, R_KEY_DIM // 2))
    _store_chunks(kr_ref, _rope_chunks(proj(o + rq, o + 2 * rq), cr_ref, sr1_ref, sr2_ref, R_KEY_DIM // 2),
                  scale=R_KEY_DIM ** -0.5)
    vr_ref[...] = proj(o + 2 * rq, o + 2 * rq + rv).astype(vr_ref.dtype)


def _in_projection(x2d, g_pre, w_a, tabs_a, tabs_r, *, tm, with_means, act_dtype, q_scale):
    t = x2d.shape[0]
    n = t // tm
    tab_blocks = tabs_a[0].shape[0] // tm
    tab_spec = pl.BlockSpec((tm, LANES), lambda i: (i % tab_blocks, 0))
    row = lambda width: pl.BlockSpec((tm, width), lambda i: (i, 0))
    const = lambda shape: pl.BlockSpec(shape, lambda i: (0,) * len(shape))
    out_shape = [
        jax.ShapeDtypeStruct((t, A_WIDTH), act_dtype),
        jax.ShapeDtypeStruct((t, A_WIDTH), F32),
        jax.ShapeDtypeStruct((t, A_WIDTH), BF16),
        jax.ShapeDtypeStruct((t, A_WIDTH), F32),
        jax.ShapeDtypeStruct((t, A_WIDTH), BF16),
        jax.ShapeDtypeStruct((t, R_QK_WIDTH), act_dtype),
        jax.ShapeDtypeStruct((t, R_QK_WIDTH), act_dtype),
        jax.ShapeDtypeStruct((t, R_V_WIDTH), act_dtype),
    ]
    out_specs = [row(A_WIDTH)] * 5 + [row(R_QK_WIDTH)] * 2 + [row(R_V_WIDTH)]
    if with_means:
        nb = tm // MOBA_BLOCK
        out_shape.append(jax.ShapeDtypeStruct((n, nb, A_WIDTH), F32))
        out_specs.append(pl.BlockSpec((1, nb, A_WIDTH), lambda i: (i, 0, 0)))
    return pl.pallas_call(
        functools.partial(_inproj_kernel, q_scale=q_scale),
        grid=(n,),
        in_specs=[row(D_MODEL), const((1, D_MODEL)), const(w_a.shape)] + [tab_spec] * 6,
        out_specs=out_specs,
        out_shape=out_shape,
        compiler_params=_params("arbitrary"),
        name="in_projection",
    )(x2d, g_pre, w_a, *tabs_a, *tabs_r)


def _moba_prompt_kernel(q_ref, k_ref, v_ref, km_ref, o_ref, vt_ref, qs_ref, bias_ref, acc_ref, m_ref,
                        s0_ref, s1_ref, s2_ref, s3_ref):
    i = pl.program_id(2)
    nb = MOBA_BLOCK
    hd = A_HEAD_DIM
    n_blocks = k_ref.shape[0] // nb

    @pl.when(i == 0)
    def _():
        ones_rows = (lax.broadcasted_iota(I32, (MOBA_V_ROWS - hd, nb), 0) == 0).astype(BF16)
        for c in range(n_blocks):
            vt = v_ref[c * nb:(c + 1) * nb, :].astype(F32).T.astype(BF16)
            for h in range(2):
                vt_ref[c, h, 0:hd, :] = vt[h * hd:(h + 1) * hd, :]
                vt_ref[c, h, hd:MOBA_V_ROWS, :] = ones_rows

    qt = q_ref[...].astype(F32).T
    row = lax.broadcasted_iota(I32, qt.shape, 0)
    km = km_ref[...]
    km_hi = km.astype(BF16)
    km_lo = (km - km_hi.astype(F32)).astype(BF16)
    jrow = lax.broadcasted_iota(I32, (n_blocks, nb), 0)
    for h in range(2):
        qb = jnp.where((row >= hd) if h else (row < hd), qt, 0.0).astype(BF16)
        qs_ref[h] = qb
        sc = (jnp.dot(km_hi, qb, preferred_element_type=F32)
              + jnp.dot(km_lo, qb, preferred_element_type=F32))
        sc = jnp.where(jrow < i, sc, NEG)
        cnt = jnp.zeros(sc.shape, I32)
        for jp in range(n_blocks):
            r = sc[jp:jp + 1, :]
            beats = (r > sc) | ((r == sc) & (jp < jrow))
            cnt = cnt + beats.astype(I32)
        sel = (jrow < i) & (cnt < MOBA_TOPK)
        bias = jnp.where(sel, 0.0, NEG)
        for jp in range(n_blocks):
            bias_ref[jp + 1, h:h + 1, :] = bias[jp:jp + 1, :]
    bias_ref[0, 0:2, :] = jnp.zeros((2, nb), F32)
    m_ref[0:2, :] = jnp.full((2, nb), NEG, F32)
    acc_ref[...] = jnp.zeros(acc_ref.shape, F32)

    key_idx = lax.broadcasted_iota(I32, (nb, nb), 0)
    qry_idx = lax.broadcasted_iota(I32, (nb, nb), 1)

    def scores_into(s_ref, blk, causal=False):
        kj = k_ref[pl.ds(pl.multiple_of(blk * nb, nb), nb), :]
        for h in range(2):
            s = jnp.dot(kj, qs_ref[h], preferred_element_type=F32)
            s_ref[h] = jnp.where(key_idx <= qry_idx, s, NEG) if causal else s

    def attend(s_ref, t, v_blk):
        quarter = nb // 4
        for h in range(2):
            brow = bias_ref[t, h:h + 1, :]
            parts = [s_ref[h, r * quarter:(r + 1) * quarter, :] for r in range(4)]
            tall = jnp.maximum(jnp.maximum(parts[0], parts[1]), jnp.maximum(parts[2], parts[3]))
            m_old = m_ref[h:h + 1, :]
            m_new = jnp.maximum(m_old, jnp.max(tall, axis=0, keepdims=True) + brow)
            alpha = jnp.exp2(m_old - m_new)
            offset = jnp.where(brow < 0.0, -NEG, m_new)
            p = jnp.exp2(s_ref[h] - offset).astype(BF16)
            pv = jnp.dot(vt_ref[v_blk, h], p, preferred_element_type=F32)
            acc_ref[h] = alpha * acc_ref[h] + pv
            m_ref[h:h + 1, :] = m_new

    bufs = (s0_ref, s1_ref, s2_ref, s3_ref)
    scores_into(bufs[0], i, causal=True)
    for r in range(1, MOBA_GROUP):
        scores_into(bufs[r], r - 1)
    attend(bufs[0], 0, i)
    for r in range(1, MOBA_GROUP):
        attend(bufs[r], r, r - 1)

    def group_body(g, carry):
        base = MOBA_GROUP * g - 1
        for r in range(MOBA_GROUP):
            scores_into(bufs[r], base + r)
        for r in range(MOBA_GROUP):
            attend(bufs[r], base + r + 1, base + r)
        return carry

    lax.fori_loop(1, i // MOBA_GROUP + 1, group_body, 0)

    outs = [acc_ref[h, 0:hd, :] / acc_ref[h, hd:hd + 1, :] for h in range(2)]
    o_ref[...] = jnp.concatenate(outs, axis=0).T.astype(o_ref.dtype)


def _moba_prompt(q, k, v, km, *, batch, seq):
    nb = MOBA_BLOCK
    n_blocks = seq // nb
    pairs = A_HEADS // 2
    pw = 2 * A_HEAD_DIM
    return pl.pallas_call(
        _moba_prompt_kernel,
        grid=(batch, pairs, n_blocks),
        in_specs=[
            pl.BlockSpec((nb, pw), lambda b, p, i: (b * n_blocks + i, p)),
            pl.BlockSpec((seq, pw), lambda b, p, i: (b, p)),
            pl.BlockSpec((seq, pw), lambda b, p, i: (b, p)),
            pl.BlockSpec((n_blocks, pw), lambda b, p, i: (b, p)),
        ],
        out_specs=pl.BlockSpec((nb, pw), lambda b, p, i: (b * n_blocks + i, p)),
        out_shape=jax.ShapeDtypeStruct((batch * seq, A_WIDTH), BF16),
        scratch_shapes=[
            pltpu.VMEM((n_blocks, 2, MOBA_V_ROWS, nb), BF16),
            pltpu.VMEM((2, pw, nb), BF16),
            pltpu.VMEM((n_blocks + 1, SUBLANES, nb), F32),
            pltpu.VMEM((2, MOBA_V_ROWS, nb), F32),
            pltpu.VMEM((SUBLANES, nb), F32),
        ] + [pltpu.VMEM((2, nb, nb), F32)] * MOBA_GROUP,
        compiler_params=_params("arbitrary", "arbitrary", "arbitrary"),
        name="moba_prompt",
    )(q, k, v, km)


def _retention_prompt_kernel(q_ref, k_ref, v_ref, dmat_ref, qdec_ref, kdec_ref, y_ref, rfin_ref, r_ref,
                             *, state_decay):
    c = pl.program_id(1)
    kd, vd = R_KEY_DIM, R_VAL_DIM

    @pl.when(c == 0)
    def _():
        r_ref[...] = jnp.zeros(r_ref.shape, F32)

    chunk = q_ref.shape[0]
    lane = lax.broadcasted_iota(I32, (chunk, 2 * kd), 1)
    for p in range(R_HEADS // 2):
        q2 = q_ref[:, p * 2 * kd:(p + 1) * 2 * kd].astype(F32)
        k2 = k_ref[:, p * 2 * kd:(p + 1) * 2 * kd]
        qd2 = q2 * qdec_ref[p]
        kdt = (k2.astype(F32) * kdec_ref[p]).T.astype(BF16)
        r2 = r_ref[p]
        r2b = r2.astype(BF16)
        for hl in range(2):
            h = 2 * p + hl
            own = (lane >= kd) == bool(hl)
            qm = jnp.where(own, q2, 0.0).astype(BF16)
            qdm = jnp.where(own, qd2, 0.0).astype(BF16)
            sc = lax.dot_general(qm, k2, (((1,), (1,)), ((), ())), preferred_element_type=F32) * dmat_ref[h]
            vh = v_ref[:, h * vd:(h + 1) * vd]
            o = (jnp.dot(sc.astype(BF16), vh, preferred_element_type=F32)
                 + jnp.dot(qdm, r2b, preferred_element_type=F32))
            sl = slice(hl * kd, (hl + 1) * kd)
            u = jnp.dot(kdt[sl, :], vh, preferred_element_type=F32)
            r_ref[p, sl, :] = state_decay[h] * r2[sl, :] + u
            y = o * lax.rsqrt(jnp.mean(o * o, axis=-1, keepdims=True) + NORM_EPS)
            y_ref[:, h * vd:(h + 1) * vd] = y.astype(y_ref.dtype)

    @pl.when(c == pl.num_programs(1) - 1)
    def _():
        for h in range(R_HEADS):
            rfin_ref[0, h] = r_ref[h // 2, (h % 2) * kd:(h % 2 + 1) * kd, :]


def _retention_prompt(qr, kr, vr, *, batch, seq):
    chunk = RET_CHUNK
    nc = seq // chunk
    lg = _retention_log_decay()
    idx = jnp.arange(chunk, dtype=F32)
    diff = idx[:, None] - idx[None, :]
    dmat = jnp.where(diff >= 0, jnp.exp(lg[:, None, None] * jnp.maximum(diff, 0.0)), 0.0)
    qdec = jnp.exp(lg[:, None] * (idx + 1.0))
    kdec = jnp.exp(lg[:, None] * (chunk - 1.0 - idx))
    pair_table = lambda t: jnp.repeat(t.reshape(R_HEADS // 2, 2, chunk), R_KEY_DIM, axis=1).transpose(0, 2, 1)
    lg32 = np.log(np.float32(1.0) - np.exp2(np.float32(-5.0) - np.arange(R_HEADS, dtype=np.float32)))
    state_decay = tuple(float(np.exp(lg32[h] * np.float32(chunk))) for h in range(R_HEADS))
    const = lambda shape: pl.BlockSpec(shape, lambda b, c: (0,) * len(shape))
    return pl.pallas_call(
        functools.partial(_retention_prompt_kernel, state_decay=state_decay),
        grid=(batch, nc),
        in_specs=[
            pl.BlockSpec((chunk, R_QK_WIDTH), lambda b, c: (b * nc + c, 0)),
            pl.BlockSpec((chunk, R_QK_WIDTH), lambda b, c: (b * nc + c, 0)),
            pl.BlockSpec((chunk, R_V_WIDTH), lambda b, c: (b * nc + c, 0)),
            const((R_HEADS, chunk, chunk)),
            const((R_HEADS // 2, chunk, 2 * R_KEY_DIM)),
            const((R_HEADS // 2, chunk, 2 * R_KEY_DIM)),
        ],
        out_specs=[
            pl.BlockSpec((chunk, R_V_WIDTH), lambda b, c: (b * nc + c, 0)),
            pl.BlockSpec((1, R_HEADS, R_KEY_DIM, R_VAL_DIM), lambda b, c: (b, 0, 0, 0)),
        ],
        out_shape=[
            jax.ShapeDtypeStruct((batch * seq, R_V_WIDTH), BF16),
            jax.ShapeDtypeStruct((batch, R_HEADS, R_KEY_DIM, R_VAL_DIM), F32),
        ],
        scratch_shapes=[pltpu.VMEM((R_HEADS // 2, 2 * R_KEY_DIM, R_VAL_DIM), F32)],
        compiler_params=_params("arbitrary", "arbitrary"),
        name="retention_prompt",
    )(qr, kr, vr, dmat, pair_table(qdec), pair_table(kdec))


def _merge_kernel(x_ref, ya_ref, yr_ref, g_ref, wg_ref, wm_ref, b_ref, wpa_ref, wpr_ref, wo_ref, gp_ref, y_ref):
    x = x_ref[...]
    h = _rms_normed(x, g_ref[...]).astype(BF16)
    gates = jnp.dot(h, wg_ref[...], preferred_element_type=F32)
    ua = (ya_ref[...].astype(F32) * jax.nn.silu(gates[:, :A_WIDTH])).astype(BF16)
    ur = (yr_ref[...].astype(F32) * jax.nn.silu(gates[:, A_WIDTH:])).astype(BF16)
    pa = jnp.dot(ua, wpa_ref[...], preferred_element_type=F32)
    pr = jnp.dot(ur, wpr_ref[...], preferred_element_type=F32)
    mix = jnp.dot(h, wm_ref[...], preferred_element_type=F32) + b_ref[...]
    m = jax.nn.sigmoid(mix[:, :D_MODEL]) * pa + jax.nn.sigmoid(mix[:, D_MODEL:]) * pr
    o = jnp.dot(m.astype(BF16), wo_ref[...], preferred_element_type=F32)
    y_ref[...] = x + _rms_normed(o, gp_ref[...])


def _merge(x2d, ya, yr, g_pre, w_g, w_m, b_merge, w_pa, w_pr, w_out, g_post, *, tm):
    t = x2d.shape[0]
    row = lambda width: pl.BlockSpec((tm, width), lambda i: (i, 0))
    const = lambda a: pl.BlockSpec(a.shape, lambda i: (0,) * a.ndim)
    return pl.pallas_call(
        _merge_kernel,
        grid=(t // tm,),
        in_specs=[row(D_MODEL), row(A_WIDTH), row(R_V_WIDTH), const(g_pre), const(w_g), const(w_m),
                  const(b_merge), const(w_pa), const(w_pr), const(w_out), const(g_post)],
        out_specs=row(D_MODEL),
        out_shape=jax.ShapeDtypeStruct((t, D_MODEL), F32),
        compiler_params=_params("arbitrary"),
        name="gate_merge",
    )(x2d, ya, yr, g_pre, w_g, w_m, b_merge, w_pa, w_pr, w_out, g_post)


def _select_blocks_kernel(pt_ref, q_ref, *refs, n_blocks):
    del pt_ref
    pages, idx_ref, qcol_ref, part_ref = refs[:-3], refs[-3], refs[-2], refs[-1]
    c = pl.program_id(1)
    blocks_per_step = len(pages) // PAGES_PER_BLOCK
    groups = A_HEAD_DIM // SUBLANES
    lane = lax.broadcasted_iota(I32, part_ref.shape, 1)

    @pl.when(c == 0)
    def _():
        part_ref[...] = jnp.zeros(part_ref.shape, F32)
        eye = lax.broadcasted_iota(I32, (LANES, LANES), 0) == lax.broadcasted_iota(I32, (LANES, LANES), 1)
        for j in range(A_WIDTH // LANES):
            qrow = jnp.broadcast_to(q_ref[0][:, j * LANES:(j + 1) * LANES], (LANES, LANES))
            col = jnp.sum(jnp.where(eye, qrow, 0.0), axis=1, keepdims=True)
            qcol_ref[j * LANES:(j + 1) * LANES, :] = jnp.broadcast_to(col, (LANES, PAGE_SIZE))

    for r in range(blocks_per_step):
        tot = pages[r * PAGES_PER_BLOCK][0]
        for half in range(1, PAGES_PER_BLOCK):
            tot = tot + pages[r * PAGES_PER_BLOCK + half][0]
        prod = tot.reshape(A_WIDTH, PAGE_SIZE) * qcol_ref[...]
        folded = jnp.sum(prod.reshape(A_HEADS, groups, SUBLANES, PAGE_SIZE), axis=1)
        col = jnp.sum(folded.reshape(A_HEADS * SUBLANES, PAGE_SIZE), axis=1, keepdims=True)
        part_ref[...] = jnp.where(lane == c * blocks_per_step + r, col, part_ref[...])

    @pl.when(c == pl.num_programs(1) - 1)
    def _():
        sc = jnp.sum(part_ref[...].reshape(A_HEADS, SUBLANES, LANES), axis=1) * (1.0 / MOBA_BLOCK)
        blk = lax.broadcasted_iota(I32, sc.shape, 1)
        sc = jnp.where(blk < n_blocks, sc, -jnp.inf)
        out = jnp.zeros(sc.shape, I32)
        for t in range(MOBA_TOPK):
            mx = jnp.max(sc, axis=1, keepdims=True)
            first = jnp.min(jnp.where(sc == mx, blk, LANES), axis=1, keepdims=True)
            out = jnp.where(blk == t, first, out)
            sc = jnp.where(blk == first, -jnp.inf, sc)
        idx_ref[0] = out


def _select_blocks(cache_kt, pt_flat, q_s, *, batch, n_pages):
    g = PAGES_PER_STEP
    steps = n_pages // g
    n_blocks = n_pages // PAGES_PER_BLOCK

    def page_spec(r):
        return pl.BlockSpec((1, A_HEADS, A_HEAD_DIM, PAGE_SIZE),
                            lambda b, c, pt: (pt[b * n_pages + c * g + r], 0, 0, 0))

    return pl.pallas_call(
        functools.partial(_select_blocks_kernel, n_blocks=n_blocks),
        grid_spec=pltpu.PrefetchScalarGridSpec(
            num_scalar_prefetch=1,
            grid=(batch, steps),
            in_specs=[pl.BlockSpec((1, 1, A_WIDTH), lambda b, c, pt: (b, 0, 0))] + [page_spec(r) for r in range(g)],
            out_specs=pl.BlockSpec((1, A_HEADS, LANES), lambda b, c, pt: (b, 0, 0)),
            scratch_shapes=[
                pltpu.VMEM((A_WIDTH, PAGE_SIZE), F32),
                pltpu.VMEM((A_HEADS * SUBLANES, LANES), F32),
            ],
        ),
        out_shape=jax.ShapeDtypeStruct((batch, A_HEADS, LANES), I32),
        compiler_params=_params("arbitrary", "arbitrary"),
        name="select_blocks",
    )(pt_flat, q_s.reshape(batch, 1, A_WIDTH), *([cache_kt] * g))


def _sample_attn_kernel(sel_ref, pt_ref, q_ref, ks_ref, vs_ref, *refs):
    del sel_ref, pt_ref
    n_slabs = MOBA_TOPK * PAGES_PER_BLOCK
    o_ref = refs[-1]
    hd = A_HEAD_DIM
    nt = (((1,), (1,)), ((), ()))
    outs = []
    for hl in range(2):
        k_slabs = refs[hl * n_slabs:(hl + 1) * n_slabs]
        v_slabs = refs[(2 + hl) * n_slabs:(3 + hl) * n_slabs]
        cols = slice(hl * hd, (hl + 1) * hd)
        q = q_ref[0][:, cols] * (hd ** -0.5)
        q8 = jnp.broadcast_to(q, (SUBLANES, hd)).astype(BF16)
        s_self = jnp.sum(q * ks_ref[0][:, cols], axis=-1, keepdims=True)
        s = [jnp.dot(q8, kp[0, 0].astype(BF16), preferred_element_type=F32) for kp in k_slabs]
        m = s_self
        for sp in s:
            m = jnp.maximum(m, jnp.max(sp, axis=-1, keepdims=True))
        p_self = jnp.exp(s_self - m)
        l = p_self
        acc = p_self * vs_ref[0][:, cols]
        for sp, vp in zip(s, v_slabs):
            p = jnp.exp(sp - m)
            l = l + jnp.sum(p, axis=-1, keepdims=True)
            acc = acc + lax.dot_general(p.astype(BF16), vp[0, 0].astype(BF16), nt, preferred_element_type=F32)
        outs.append((acc / l)[0:1, :])
    o_ref[0] = jnp.concatenate(outs, axis=1)


def _sample_attention(sel_flat, pt_flat, q_s, k_s, v_s, cache_kt, cache_vt, *, batch, n_pages):
    pairs = A_HEADS // 2
    pw = 2 * A_HEAD_DIM

    def slab_spec(hl, t, half):
        def index_map(b, p, sel, pt):
            head = 2 * p + hl
            blk = sel[(b * A_HEADS + head) * MOBA_TOPK + t]
            return (pt[b * n_pages + blk * PAGES_PER_BLOCK + half], head, 0, 0)
        return pl.BlockSpec((1, 1, A_HEAD_DIM, PAGE_SIZE), index_map)

    slab_specs = [slab_spec(hl, t, half) for hl in range(2) for t in range(MOBA_TOPK)
                  for half in range(PAGES_PER_BLOCK)]
    tok_spec = pl.BlockSpec((1, 1, pw), lambda b, p, sel, pt: (b, 0, p))
    n_ops = len(slab_specs)
    return pl.pallas_call(
        _sample_attn_kernel,
        grid_spec=pltpu.PrefetchScalarGridSpec(
            num_scalar_prefetch=2,
            grid=(batch, pairs),
            in_specs=[tok_spec] * 3 + slab_specs + slab_specs,
            out_specs=tok_spec,
        ),
        out_shape=jax.ShapeDtypeStruct((batch, 1, A_WIDTH), F32),
        compiler_params=_params("arbitrary", "arbitrary"),
        name="sample_attention",
    )(sel_flat, pt_flat, q_s.reshape(batch, 1, A_WIDTH), k_s.reshape(batch, 1, A_WIDTH),
      v_s.reshape(batch, 1, A_WIDTH), *([cache_kt] * n_ops), *([cache_vt] * n_ops))


def _sample_retention_kernel(q_ref, k_ref, v_ref, r_ref, y_ref, rnew_ref, *, decay):
    bb = q_ref.shape[0]
    kd, vd = R_KEY_DIM, R_VAL_DIM
    reps = LANES // bb
    qt = jnp.concatenate([q_ref[...]] * reps, axis=0).T
    kt = jnp.concatenate([k_ref[...]] * reps, axis=0).T
    for bi in range(bb):
        for h in range(R_HEADS):
            rows = slice(h * kd, (h + 1) * kd)
            qc = qt[rows, bi:bi + 1]
            kc = kt[rows, bi:bi + 1]
            vrow = v_ref[bi:bi + 1, h * vd:(h + 1) * vd]
            r = r_ref[bi, rows, :]
            qk = jnp.sum(qc * kc, axis=0, keepdims=True)
            cross = jnp.sum((qc * decay[h]) * r, axis=0, keepdims=True)
            o = qk * vrow + cross
            rnew_ref[bi, rows, :] = decay[h] * r + kc * vrow
            y = o * lax.rsqrt(jnp.mean(o * o, axis=-1, keepdims=True) + NORM_EPS)
            y_ref[bi:bi + 1, h * vd:(h + 1) * vd] = y


def _sample_retention(qr_s, kr_s, vr_s, state):
    batch = qr_s.shape[0]
    bb = SAMPLE_RET_BATCH
    lg32 = np.log(np.float32(1.0) - np.exp2(np.float32(-5.0) - np.arange(R_HEADS, dtype=np.float32)))
    decay = tuple(float(np.exp(lg32[h])) for h in range(R_HEADS))
    state3 = state.reshape(batch, R_HEADS * R_KEY_DIM, R_VAL_DIM)
    y, r_new = pl.pallas_call(
        functools.partial(_sample_retention_kernel, decay=decay),
        grid=(batch // bb,),
        in_specs=[
            pl.BlockSpec((bb, R_QK_WIDTH), lambda i: (i, 0)),
            pl.BlockSpec((bb, R_QK_WIDTH), lambda i: (i, 0)),
            pl.BlockSpec((bb, R_V_WIDTH), lambda i: (i, 0)),
            pl.BlockSpec((bb, R_HEADS * R_KEY_DIM, R_VAL_DIM), lambda i: (i, 0, 0)),
        ],
        out_specs=[
            pl.BlockSpec((bb, R_V_WIDTH), lambda i: (i, 0)),
            pl.BlockSpec((bb, R_HEADS * R_KEY_DIM, R_VAL_DIM), lambda i: (i, 0, 0)),
        ],
        out_shape=[
            jax.ShapeDtypeStruct((batch, R_V_WIDTH), F32),
            jax.ShapeDtypeStruct(state3.shape, F32),
        ],
        compiler_params=_params("arbitrary"),
        name="sample_retention",
    )(qr_s, kr_s, vr_s, state3)
    return y, r_new.reshape(state.shape)


def _layer(x_p, x_s, cache_k, cache_v, state_r, page_table, g_pre, w_in, b_merge, w_pa, w_pr, w_out, g_post):
    batch, seq, _ = x_p.shape
    dec_batch, dec_seq, _ = x_s.shape
    assert dec_seq == 1
    n_pages = page_table.shape[1]
    past_len = n_pages * PAGE_SIZE
    assert seq % PROMPT_TOKENS_PER_TILE == 0 and seq % RET_CHUNK == 0 and seq % (MOBA_GROUP * MOBA_BLOCK) == 0
    assert n_pages % PAGES_PER_STEP == 0 and MOBA_TOPK <= n_pages // PAGES_PER_BLOCK <= LANES
    assert dec_batch % SAMPLE_RET_BATCH == 0

    o = _OFFS
    w_bf = w_in.astype(BF16)
    w_a = jnp.concatenate([w_bf[:, o[0]:o[3]], w_bf[:, o[4]:o[7]]], axis=1)
    w_g = jnp.concatenate([w_bf[:, o[3]:o[4]], w_bf[:, o[7]:o[8]]], axis=1)
    w_m = w_bf[:, o[8]:o[10]]
    w_pa_b, w_pr_b, w_out_b = w_pa.astype(BF16), w_pr.astype(BF16), w_out.astype(BF16)
    g_pre2, g_post2, b2 = g_pre.reshape(1, D_MODEL), g_post.reshape(1, D_MODEL), b_merge.reshape(1, 2 * D_MODEL)

    pos_p = jnp.arange(seq, dtype=I32)
    xp2 = x_p.reshape(batch * seq, D_MODEL)
    q, kf, kb, vf, vb, qr, kr, vr, km = _in_projection(
        xp2, g_pre2, w_a,
        _rope_tables(pos_p, ROPE_DIM, A_HEAD_DIM, ROPE_THETA),
        _rope_tables(pos_p, R_KEY_DIM, R_KEY_DIM, R_ROPE_THETA),
        tm=PROMPT_TOKENS_PER_TILE, with_means=True, act_dtype=BF16, q_scale=MOBA_Q_SCALE)
    km = km.reshape(batch * (seq // MOBA_BLOCK), A_WIDTH)
    ya = _moba_prompt(q, kb, vb, km, batch=batch, seq=seq)
    yr, r_p = _retention_prompt(qr, kr, vr, batch=batch, seq=seq)
    y_p = _merge(xp2, ya, yr, g_pre2, w_g, w_m, b2, w_pa_b, w_pr_b, w_out_b, g_post2, tm=PROMPT_TOKENS_PER_TILE)

    pos_s = jnp.full((dec_batch,), past_len, dtype=I32)
    xs2 = x_s.reshape(dec_batch, D_MODEL)
    q_s, k_s, _, v_s, _, qr_s, kr_s, vr_s = _in_projection(
        xs2, g_pre2, w_a,
        _rope_tables(pos_s, ROPE_DIM, A_HEAD_DIM, ROPE_THETA),
        _rope_tables(pos_s, R_KEY_DIM, R_KEY_DIM, R_ROPE_THETA),
        tm=dec_batch, with_means=False, act_dtype=F32, q_scale=None)
    cache_kt = cache_k.transpose(0, 2, 3, 1)
    cache_vt = cache_v.transpose(0, 2, 3, 1)
    pt_flat = page_table.reshape(-1)
    sel = _select_blocks(cache_kt, pt_flat, q_s, batch=dec_batch, n_pages=n_pages)
    sel_flat = sel[:, :, :MOBA_TOPK].reshape(-1)
    ya_s = _sample_attention(sel_flat, pt_flat, q_s, k_s, v_s, cache_kt, cache_vt,
                             batch=dec_batch, n_pages=n_pages).reshape(dec_batch, A_WIDTH)
    yr_s, r_s = _sample_retention(qr_s, kr_s, vr_s, state_r)
    y_s = _merge(xs2, ya_s, yr_s, g_pre2, w_g, w_m, b2, w_pa_b, w_pr_b, w_out_b, g_post2, tm=dec_batch)

    return (y_p.reshape(x_p.shape), y_s.reshape(x_s.shape),
            kf.reshape(batch, seq, A_HEADS, A_HEAD_DIM), vf.reshape(batch, seq, A_HEADS, A_HEAD_DIM), r_p,
            k_s.reshape(dec_batch, 1, A_HEADS, A_HEAD_DIM), v_s.reshape(dec_batch, 1, A_HEADS, A_HEAD_DIM), r_s)


def kernel(x_prompt, x_sample, cache_k, cache_v, state_ret, page_table, norm_pre_g, w_in, b_merge, w_proj_a,
           w_proj_r, w_out, norm_post_g):
    h_p, h_s = x_prompt, x_sample
    outs = [[] for _ in range(6)]
    for layer in range(w_in.shape[0]):
        res = _layer(h_p, h_s, cache_k[layer], cache_v[layer], state_ret[layer], page_table,
                     norm_pre_g[layer], w_in[layer], b_merge[layer], w_proj_a[layer], w_proj_r[layer],
                     w_out[layer], norm_post_g[layer])
        h_p, h_s = res[0], res[1]
        for acc, leaf in zip(outs, res[2:]):
            acc.append(leaf)
    return (h_p, h_s) + tuple(jnp.stack(o) for o in outs)
```

```python
import functools

import numpy as np
import jax
import jax.numpy as jnp
from jax import lax
from jax.experimental import pallas as pl
from jax.experimental.pallas import tpu as pltpu

F32 = jnp.float32
BF16 = jnp.bfloat16
I32 = jnp.int32

D_MODEL = 1024
A_HEADS = 8
A_HEAD_DIM = 64
A_WIDTH = A_HEADS * A_HEAD_DIM
MOBA_BLOCK = 256
MOBA_TOPK = 3
ROPE_THETA = 500000.0
ROPE_DIM = A_HEAD_DIM // 4
R_HEADS = 4
R_KEY_DIM = 64
R_VAL_DIM = 128
R_QK_WIDTH = R_HEADS * R_KEY_DIM
R_V_WIDTH = R_HEADS * R_VAL_DIM
R_ROPE_THETA = 10000.0
PAGE_SIZE = 128
NORM_EPS = 1e-6
NEG = -1e30

_OFFS = np.cumsum((0, A_WIDTH, A_WIDTH, A_WIDTH, A_WIDTH, R_QK_WIDTH, R_QK_WIDTH, R_V_WIDTH, R_V_WIDTH, D_MODEL, D_MODEL))

LANES = 128
SUBLANES = 8
VMEM_LIMIT_BYTES = 56 * 1024 * 1024

MOBA_Q_SCALE = A_HEAD_DIM ** -0.5 * float(np.log2(np.e))
MOBA_GROUP = 4
BF16_SUBLANES = 16
MOBA_V_ROWS = A_HEAD_DIM + BF16_SUBLANES
RET_CHUNK = 256
PROMPT_TOKENS_PER_TILE = 512
PAGES_PER_BLOCK = MOBA_BLOCK // PAGE_SIZE
SAMPLE_SLABS = MOBA_TOPK * PAGES_PER_BLOCK
SAMPLE_RET_BATCH = 16


def _params(*semantics):
    return pltpu.CompilerParams(dimension_semantics=semantics, vmem_limit_bytes=VMEM_LIMIT_BYTES)


def _rms_normed(x, g):
    return x * lax.rsqrt(jnp.mean(x * x, axis=-1, keepdims=True) + NORM_EPS) * g


def _rope_tables(pos, rot_dim, head_dim, theta):
    half = rot_dim // 2
    inv_freq = theta ** (-jnp.arange(half, dtype=F32) * (2.0 / rot_dim))
    ang = pos.astype(F32)[:, None] * inv_freq[None, :]
    cos, sin = jnp.cos(ang), jnp.sin(ang)
    p = pos.shape[0]
    rest = head_dim - rot_dim
    zh = jnp.zeros((p, half), F32)
    c = jnp.concatenate([cos, cos, jnp.ones((p, rest), F32)], axis=1)
    s1 = jnp.concatenate([-sin, zh, jnp.zeros((p, rest), F32)], axis=1)
    s2 = jnp.concatenate([zh, sin, jnp.zeros((p, rest), F32)], axis=1)
    reps = LANES // head_dim
    return tuple(jnp.tile(t, (1, reps)) for t in (c, s1, s2))


def _retention_log_decay():
    return jnp.log(1.0 - jnp.exp2(-5.0 - jnp.arange(R_HEADS, dtype=F32)))


def _rope_chunks(z, c_ref, s1_ref, s2_ref, half):
    c, s1, s2 = c_ref[...], s1_ref[...], s2_ref[...]
    out = []
    for j in range(z.shape[1] // LANES):
        zs = z[:, j * LANES:(j + 1) * LANES]
        up = pltpu.roll(zs, LANES - half, 1)
        down = pltpu.roll(zs, half, 1)
        out.append(zs * c + up * s1 + down * s2)
    return out


def _store_chunks(ref, chunks, scale=None):
    for j, ch in enumerate(chunks):
        if scale is not None:
            ch = ch * scale
        ref[:, j * LANES:(j + 1) * LANES] = ch.astype(ref.dtype)


def _store_chunks_token_minor(refs, chunks):
    for j, ch in enumerate(chunks):
        ct = ch.T
        for ref in refs:
            ref[0, j * LANES:(j + 1) * LANES, :] = ct.astype(ref.dtype)


def _inproj_kernel(x_ref, g_ref, w_ref, ca_ref, sa1_ref, sa2_ref, cr_ref, sr1_ref, sr2_ref,
                   q_ref, kf_ref, kb_ref, vf_ref, vb_ref, qr_ref, kr_ref, vr_ref, *km_refs, q_scale, token_minor_kv):
    h = _rms_normed(x_ref[...], g_ref[...]).astype(BF16)

    def proj(lo, hi):
        return jnp.dot(h, w_ref[:, lo:hi], preferred_element_type=F32)

    a, rq, rv = A_WIDTH, R_QK_WIDTH, R_V_WIDTH
    _store_chunks(q_ref, _rope_chunks(proj(0, a), ca_ref, sa1_ref, sa2_ref, ROPE_DIM // 2), scale=q_scale)
    k_chunks = _rope_chunks(proj(a, 2 * a), ca_ref, sa1_ref, sa2_ref, ROPE_DIM // 2)
    if token_minor_kv:
        _store_chunks_token_minor([kf_ref], k_chunks)
    else:
        _store_chunks(kf_ref, k_chunks)
    _store_chunks(kb_ref, k_chunks)
    if km_refs:
        (km_ref,) = km_refs
        for j, ch in enumerate(k_chunks):
            for blk in range(ch.shape[0] // MOBA_BLOCK):
                rows = ch[blk * MOBA_BLOCK:(blk + 1) * MOBA_BLOCK, :]
                km_ref[0, blk:blk + 1, j * LANES:(j + 1) * LANES] = (
                    jnp.sum(rows, axis=0, keepdims=True) * (1.0 / MOBA_BLOCK))
    v = proj(2 * a, 3 * a)
    if token_minor_kv:
        _store_chunks_token_minor([vf_ref, vb_ref], [v[:, j * LANES:(j + 1) * LANES] for j in range(a // LANES)])
    else:
        vf_ref[...] = v
        vb_ref[...] = v.astype(vb_ref.dtype)
    o = 3 * a
    _store_chunks(qr_ref, _rope_chunks(proj(o, o + rq), cr_ref, sr1_ref, sr2_ref, R_KEY_DIM // 2))
    _store_chunks(kr_ref, _rope_chunks(proj(o + rq, o + 2 * rq), cr_ref, sr1_ref, sr2_ref, R_KEY_DIM // 2),
                  scale=R_KEY_DIM ** -0.5)
    vr_ref[...] = proj(o + 2 * rq, o + 2 * rq + rv).astype(vr_ref.dtype)


def _in_projection(x2d, g_pre, w_a, tabs_a, tabs_r, *, tm, prompt_seq, act_dtype, q_scale):
    t = x2d.shape[0]
    n = t // tm
    tab_blocks = tabs_a[0].shape[0] // tm
    tab_spec = pl.BlockSpec((tm, LANES), lambda i: (i % tab_blocks, 0))
    row = lambda width: pl.BlockSpec((tm, width), lambda i: (i, 0))
    const = lambda shape: pl.BlockSpec(shape, lambda i: (0,) * len(shape))
    if prompt_seq is None:
        kv_shape, kv_spec = (t, A_WIDTH), row(A_WIDTH)
    else:
        seq_blocks = prompt_seq // tm
        kv_shape = (t // prompt_seq, A_WIDTH, prompt_seq)
        kv_spec = pl.BlockSpec((1, A_WIDTH, tm), lambda i: (i // seq_blocks, 0, i % seq_blocks))
    out_shape = [
        jax.ShapeDtypeStruct((t, A_WIDTH), act_dtype),
        jax.ShapeDtypeStruct(kv_shape, F32),
        jax.ShapeDtypeStruct((t, A_WIDTH), BF16),
        jax.ShapeDtypeStruct(kv_shape, F32),
        jax.ShapeDtypeStruct(kv_shape, BF16),
        jax.ShapeDtypeStruct((t, R_QK_WIDTH), act_dtype),
        jax.ShapeDtypeStruct((t, R_QK_WIDTH), act_dtype),
        jax.ShapeDtypeStruct((t, R_V_WIDTH), act_dtype),
    ]
    out_specs = [row(A_WIDTH), kv_spec, row(A_WIDTH), kv_spec, kv_spec] + [row(R_QK_WIDTH)] * 2 + [row(R_V_WIDTH)]
    if prompt_seq is not None:
        nb = tm // MOBA_BLOCK
        out_shape.append(jax.ShapeDtypeStruct((n, nb, A_WIDTH), F32))
        out_specs.append(pl.BlockSpec((1, nb, A_WIDTH), lambda i: (i, 0, 0)))
    return pl.pallas_call(
        functools.partial(_inproj_kernel, q_scale=q_scale, token_minor_kv=prompt_seq is not None),
        grid=(n,),
        in_specs=[row(D_MODEL), const((1, D_MODEL)), const(w_a.shape)] + [tab_spec] * 6,
        out_specs=out_specs,
        out_shape=out_shape,
        compiler_params=_params("arbitrary"),
        name="in_projection",
    )(x2d, g_pre, w_a, *tabs_a, *tabs_r)


def _moba_prompt_kernel(pt_ref, q_ref, k_ref, v_ref, km_ref, qsel_ref, *refs, n_sel_pages, sel_steps, sel_blocks):
    del pt_ref
    pages = refs[:n_sel_pages]
    (o_ref, score_ref, vt_ref, qs_ref, bias_ref, acc_ref, m_ref, qcol_ref, part_ref,
     s0_ref, s1_ref, s2_ref, s3_ref) = refs[n_sel_pages:]
    i = pl.program_id(2)
    nb = MOBA_BLOCK
    hd = A_HEAD_DIM
    n_blocks = k_ref.shape[0] // nb

    step = (pl.program_id(0) * pl.num_programs(1) + pl.program_id(1)) * pl.num_programs(2) + i
    sel_c = step % sel_steps
    _sample_block_scores_step(sel_c, sel_c == sel_steps - 1, qsel_ref, pages, score_ref, qcol_ref, part_ref,
                              sel_blocks)

    @pl.when(i == 0)
    def _():
        ones_rows = (lax.broadcasted_iota(I32, (MOBA_V_ROWS - hd, nb), 0) == 0).astype(BF16)
        for c in range(n_blocks):
            for h in range(2):
                vt_ref[c, h, 0:hd, :] = v_ref[0, h * hd:(h + 1) * hd, c * nb:(c + 1) * nb]
                vt_ref[c, h, hd:MOBA_V_ROWS, :] = ones_rows

    qt = q_ref[...].astype(F32).T
    row = lax.broadcasted_iota(I32, qt.shape, 0)
    km = km_ref[...]
    km_hi = km.astype(BF16)
    km_lo = (km - km_hi.astype(F32)).astype(BF16)
    jrow = lax.broadcasted_iota(I32, (n_blocks, nb), 0)
    for h in range(2):
        qb = jnp.where((row >= hd) if h else (row < hd), qt, 0.0).astype(BF16)
        qs_ref[h] = qb
        sc = (jnp.dot(km_hi, qb, preferred_element_type=F32)
              + jnp.dot(km_lo, qb, preferred_element_type=F32))
        sc = jnp.where(jrow < i, sc, NEG)
        cnt = jnp.zeros(sc.shape, I32)
        for jp in range(n_blocks):
            r = sc[jp:jp + 1, :]
            beats = (r > sc) | ((r == sc) & (jp < jrow))
            cnt = cnt + beats.astype(I32)
        sel = (jrow < i) & (cnt < MOBA_TOPK)
        bias = jnp.where(sel, 0.0, NEG)
        for jp in range(n_blocks):
            bias_ref[jp + 1, h:h + 1, :] = bias[jp:jp + 1, :]
    bias_ref[0, 0:2, :] = jnp.zeros((2, nb), F32)
    m_ref[0:2, :] = jnp.full((2, nb), NEG, F32)
    acc_ref[...] = jnp.zeros(acc_ref.shape, F32)

    key_idx = lax.broadcasted_iota(I32, (nb, nb), 0)
    qry_idx = lax.broadcasted_iota(I32, (nb, nb), 1)

    def scores_into(s_ref, blk, causal=False):
        kj = k_ref[pl.ds(pl.multiple_of(blk * nb, nb), nb), :]
        for h in range(2):
            s = jnp.dot(kj, qs_ref[h], preferred_element_type=F32)
            s_ref[h] = jnp.where(key_idx <= qry_idx, s, NEG) if causal else s

    def attend(s_ref, t, v_blk):
        quarter = nb // 4
        for h in range(2):
            brow = bias_ref[t, h:h + 1, :]
            parts = [s_ref[h, r * quarter:(r + 1) * quarter, :] for r in range(4)]
            tall = jnp.maximum(jnp.maximum(parts[0], parts[1]), jnp.maximum(parts[2], parts[3]))
            m_old = m_ref[h:h + 1, :]
            m_new = jnp.maximum(m_old, jnp.max(tall, axis=0, keepdims=True) + brow)
            alpha = jnp.exp2(m_old - m_new)
            offset = jnp.where(brow < 0.0, -NEG, m_new)
            p = jnp.exp2(s_ref[h] - offset).astype(BF16)
            pv = jnp.dot(vt_ref[v_blk, h], p, preferred_element_type=F32)
            acc_ref[h] = alpha * acc_ref[h] + pv
            m_ref[h:h + 1, :] = m_new

    bufs = (s0_ref, s1_ref, s2_ref, s3_ref)
    scores_into(bufs[0], i, causal=True)
    for r in range(1, MOBA_GROUP):
        scores_into(bufs[r], r - 1)
    attend(bufs[0], 0, i)
    for r in range(1, MOBA_GROUP):
        attend(bufs[r], r, r - 1)

    def group_body(g, carry):
        base = MOBA_GROUP * g - 1
        for r in range(MOBA_GROUP):
            scores_into(bufs[r], base + r)
        for r in range(MOBA_GROUP):
            attend(bufs[r], base + r + 1, base + r)
        return carry

    lax.fori_loop(1, i // MOBA_GROUP + 1, group_body, 0)

    outs = [acc_ref[h, 0:hd, :] / acc_ref[h, hd:hd + 1, :] for h in range(2)]
    o_ref[...] = jnp.concatenate(outs, axis=0).T.astype(o_ref.dtype)


def _moba_prompt_and_sample_select(q, k, v, km, q_s, cache_kt, pt_flat, *, batch, seq, dec_batch, n_pages):
    nb = MOBA_BLOCK
    n_blocks = seq // nb
    pairs = A_HEADS // 2
    pw = 2 * A_HEAD_DIM
    n_steps = batch * pairs * n_blocks
    pages_per_step = dec_batch * n_pages // n_steps
    assert pages_per_step * n_steps == dec_batch * n_pages, "sample key pages must split evenly over the grid"
    assert pages_per_step % PAGES_PER_BLOCK == 0 and n_pages % pages_per_step == 0
    sel_steps = n_pages // pages_per_step

    def step_of(b, p, i):
        return (b * pairs + p) * n_blocks + i

    def page_spec(r):
        return pl.BlockSpec((1, A_HEADS, A_HEAD_DIM, PAGE_SIZE),
                            lambda b, p, i, pt: (pt[step_of(b, p, i) * pages_per_step + r], 0, 0, 0))

    return pl.pallas_call(
        functools.partial(_moba_prompt_kernel, n_sel_pages=pages_per_step, sel_steps=sel_steps,
                          sel_blocks=n_pages // PAGES_PER_BLOCK),
        grid_spec=pltpu.PrefetchScalarGridSpec(
            num_scalar_prefetch=1,
            grid=(batch, pairs, n_blocks),
            in_specs=[
                pl.BlockSpec((nb, pw), lambda b, p, i, pt: (b * n_blocks + i, p)),
                pl.BlockSpec((seq, pw), lambda b, p, i, pt: (b, p)),
                pl.BlockSpec((1, pw, seq), lambda b, p, i, pt: (b, p, 0)),
                pl.BlockSpec((n_blocks, pw), lambda b, p, i, pt: (b, p)),
                pl.BlockSpec((1, 1, A_WIDTH), lambda b, p, i, pt: (step_of(b, p, i) // sel_steps, 0, 0)),
            ] + [page_spec(r) for r in range(pages_per_step)],
            out_specs=[
                pl.BlockSpec((nb, pw), lambda b, p, i, pt: (b * n_blocks + i, p)),
                pl.BlockSpec((1, A_HEADS, LANES), lambda b, p, i, pt: (step_of(b, p, i) // sel_steps, 0, 0)),
            ],
            scratch_shapes=[
                pltpu.VMEM((n_blocks, 2, MOBA_V_ROWS, nb), BF16),
                pltpu.VMEM((2, pw, nb), BF16),
                pltpu.VMEM((n_blocks + 1, SUBLANES, nb), F32),
                pltpu.VMEM((2, MOBA_V_ROWS, nb), F32),
                pltpu.VMEM((SUBLANES, nb), F32),
                pltpu.VMEM((A_WIDTH, PAGE_SIZE), F32),
                pltpu.VMEM((A_HEADS * SUBLANES, LANES), F32),
            ] + [pltpu.VMEM((2, nb, nb), F32)] * MOBA_GROUP,
        ),
        out_shape=[
            jax.ShapeDtypeStruct((batch * seq, A_WIDTH), BF16),
            jax.ShapeDtypeStruct((dec_batch, A_HEADS, LANES), F32),
        ],
        compiler_params=_params("arbitrary", "arbitrary", "arbitrary"),
        name="moba_prompt",
    )(pt_flat, q, k, v, km, q_s.reshape(dec_batch, 1, A_WIDTH), *([cache_kt] * pages_per_step))


def _retention_prompt_kernel(q_ref, k_ref, v_ref, dmat_ref, qdec_ref, kdec_ref, y_ref, rfin_ref, r_ref,
                             *, state_decay):
    c = pl.program_id(1)
    kd, vd = R_KEY_DIM, R_VAL_DIM

    @pl.when(c == 0)
    def _():
        r_ref[...] = jnp.zeros(r_ref.shape, F32)

    chunk = q_ref.shape[0]
    lane = lax.broadcasted_iota(I32, (chunk, 2 * kd), 1)
    for p in range(R_HEADS // 2):
        q2 = q_ref[:, p * 2 * kd:(p + 1) * 2 * kd].astype(F32)
        k2 = k_ref[:, p * 2 * kd:(p + 1) * 2 * kd]
        qd2 = q2 * qdec_ref[p]
        kdt = (k2.astype(F32) * kdec_ref[p]).T.astype(BF16)
        r2 = r_ref[p]
        r2b = r2.astype(BF16)
        for hl in range(2):
            h = 2 * p + hl
            own = (lane >= kd) == bool(hl)
            qm = jnp.where(own, q2, 0.0).astype(BF16)
            qdm = jnp.where(own, qd2, 0.0).astype(BF16)
            sc = lax.dot_general(qm, k2, (((1,), (1,)), ((), ())), preferred_element_type=F32) * dmat_ref[h]
            vh = v_ref[:, h * vd:(h + 1) * vd]
            o = (jnp.dot(sc.astype(BF16), vh, preferred_element_type=F32)
                 + jnp.dot(qdm, r2b, preferred_element_type=F32))
            sl = slice(hl * kd, (hl + 1) * kd)
            u = jnp.dot(kdt[sl, :], vh, preferred_element_type=F32)
            r_ref[p, sl, :] = state_decay[h] * r2[sl, :] + u
            y = o * lax.rsqrt(jnp.mean(o * o, axis=-1, keepdims=True) + NORM_EPS)
            y_ref[:, h * vd:(h + 1) * vd] = y.astype(y_ref.dtype)

    @pl.when(c == pl.num_programs(1) - 1)
    def _():
        for h in range(R_HEADS):
            rfin_ref[0, h] = r_ref[h // 2, (h % 2) * kd:(h % 2 + 1) * kd, :]


def _retention_prompt(qr, kr, vr, *, batch, seq):
    chunk = RET_CHUNK
    nc = seq // chunk
    lg = _retention_log_decay()
    idx = jnp.arange(chunk, dtype=F32)
    diff = idx[:, None] - idx[None, :]
    dmat = jnp.where(diff >= 0, jnp.exp(lg[:, None, None] * jnp.maximum(diff, 0.0)), 0.0)
    qdec = jnp.exp(lg[:, None] * (idx + 1.0))
    kdec = jnp.exp(lg[:, None] * (chunk - 1.0 - idx))
    pair_table = lambda t: jnp.repeat(t.reshape(R_HEADS // 2, 2, chunk), R_KEY_DIM, axis=1).transpose(0, 2, 1)
    lg32 = np.log(np.float32(1.0) - np.exp2(np.float32(-5.0) - np.arange(R_HEADS, dtype=np.float32)))
    state_decay = tuple(float(np.exp(lg32[h] * np.float32(chunk))) for h in range(R_HEADS))
    const = lambda shape: pl.BlockSpec(shape, lambda b, c: (0,) * len(shape))
    return pl.pallas_call(
        functools.partial(_retention_prompt_kernel, state_decay=state_decay),
        grid=(batch, nc),
        in_specs=[
            pl.BlockSpec((chunk, R_QK_WIDTH), lambda b, c: (b * nc + c, 0)),
            pl.BlockSpec((chunk, R_QK_WIDTH), lambda b, c: (b * nc + c, 0)),
            pl.BlockSpec((chunk, R_V_WIDTH), lambda b, c: (b * nc + c, 0)),
            const((R_HEADS, chunk, chunk)),
            const((R_HEADS // 2, chunk, 2 * R_KEY_DIM)),
            const((R_HEADS // 2, chunk, 2 * R_KEY_DIM)),
        ],
        out_specs=[
            pl.BlockSpec((chunk, R_V_WIDTH), lambda b, c: (b * nc + c, 0)),
            pl.BlockSpec((1, R_HEADS, R_KEY_DIM, R_VAL_DIM), lambda b, c: (b, 0, 0, 0)),
        ],
        out_shape=[
            jax.ShapeDtypeStruct((batch * seq, R_V_WIDTH), BF16),
            jax.ShapeDtypeStruct((batch, R_HEADS, R_KEY_DIM, R_VAL_DIM), F32),
        ],
        scratch_shapes=[pltpu.VMEM((R_HEADS // 2, 2 * R_KEY_DIM, R_VAL_DIM), F32)],
        compiler_params=_params("arbitrary", "arbitrary"),
        name="retention_prompt",
    )(qr, kr, vr, dmat, pair_table(qdec), pair_table(kdec))


def _merge_kernel(x_ref, ya_ref, yr_ref, g_ref, wg_ref, wm_ref, b_ref, wpa_ref, wpr_ref, wo_ref, gp_ref, y_ref):
    x = x_ref[...]
    h = _rms_normed(x, g_ref[...]).astype(BF16)
    gates = jnp.dot(h, wg_ref[...], preferred_element_type=F32)
    ua = (ya_ref[...].astype(F32) * jax.nn.silu(gates[:, :A_WIDTH])).astype(BF16)
    ur = (yr_ref[...].astype(F32) * jax.nn.silu(gates[:, A_WIDTH:])).astype(BF16)
    pa = jnp.dot(ua, wpa_ref[...], preferred_element_type=F32)
    pr = jnp.dot(ur, wpr_ref[...], preferred_element_type=F32)
    mix = jnp.dot(h, wm_ref[...], preferred_element_type=F32) + b_ref[...]
    m = jax.nn.sigmoid(mix[:, :D_MODEL]) * pa + jax.nn.sigmoid(mix[:, D_MODEL:]) * pr
    o = jnp.dot(m.astype(BF16), wo_ref[...], preferred_element_type=F32)
    y_ref[...] = x + _rms_normed(o, gp_ref[...])


def _merge(x2d, ya, yr, g_pre, w_g, w_m, b_merge, w_pa, w_pr, w_out, g_post, *, tm):
    t = x2d.shape[0]
    row = lambda width: pl.BlockSpec((tm, width), lambda i: (i, 0))
    const = lambda a: pl.BlockSpec(a.shape, lambda i: (0,) * a.ndim)
    return pl.pallas_call(
        _merge_kernel,
        grid=(t // tm,),
        in_specs=[row(D_MODEL), row(A_WIDTH), row(R_V_WIDTH), const(g_pre), const(w_g), const(w_m),
                  const(b_merge), const(w_pa), const(w_pr), const(w_out), const(g_post)],
        out_specs=row(D_MODEL),
        out_shape=jax.ShapeDtypeStruct((t, D_MODEL), F32),
        compiler_params=_params("arbitrary"),
        name="gate_merge",
    )(x2d, ya, yr, g_pre, w_g, w_m, b_merge, w_pa, w_pr, w_out, g_post)


def _sample_block_scores_step(c, is_last, q_ref, pages, score_ref, qcol_ref, part_ref, n_blocks):
    blocks_per_step = len(pages) // PAGES_PER_BLOCK
    groups = A_HEAD_DIM // SUBLANES
    lane = lax.broadcasted_iota(I32, part_ref.shape, 1)

    @pl.when(c == 0)
    def _():
        part_ref[...] = jnp.zeros(part_ref.shape, F32)
        eye = lax.broadcasted_iota(I32, (LANES, LANES), 0) == lax.broadcasted_iota(I32, (LANES, LANES), 1)
        for j in range(A_WIDTH // LANES):
            qrow = jnp.broadcast_to(q_ref[0][:, j * LANES:(j + 1) * LANES], (LANES, LANES))
            col = jnp.sum(jnp.where(eye, qrow, 0.0), axis=1, keepdims=True)
            qcol_ref[j * LANES:(j + 1) * LANES, :] = jnp.broadcast_to(col, (LANES, PAGE_SIZE))

    for r in range(blocks_per_step):
        tot = pages[r * PAGES_PER_BLOCK][0]
        for half in range(1, PAGES_PER_BLOCK):
            tot = tot + pages[r * PAGES_PER_BLOCK + half][0]
        prod = tot.reshape(A_WIDTH, PAGE_SIZE) * qcol_ref[...]
        folded = jnp.sum(prod.reshape(A_HEADS, groups, SUBLANES, PAGE_SIZE), axis=1)
        col = jnp.sum(folded.reshape(A_HEADS * SUBLANES, PAGE_SIZE), axis=1, keepdims=True)
        part_ref[...] = jnp.where(lane == c * blocks_per_step + r, col, part_ref[...])

    @pl.when(is_last)
    def _():
        sc = jnp.sum(part_ref[...].reshape(A_HEADS, SUBLANES, LANES), axis=1) * (1.0 / MOBA_BLOCK)
        blk = lax.broadcasted_iota(I32, sc.shape, 1)
        score_ref[0] = jnp.where(blk < n_blocks, sc, -jnp.inf)


def _topk_kernel(score_ref, idx_ref):
    sc = score_ref[...]
    blk = lax.broadcasted_iota(I32, sc.shape, 1)
    out = jnp.zeros(sc.shape, I32)
    for t in range(MOBA_TOPK):
        mx = jnp.max(sc, axis=1, keepdims=True)
        first = jnp.min(jnp.where(sc == mx, blk, LANES), axis=1, keepdims=True)
        out = jnp.where(blk == t, first, out)
        sc = jnp.where(blk == first, -jnp.inf, sc)
    idx_ref[...] = out


def _select_topk(scores):
    rows = scores.shape[0] * scores.shape[1]
    return pl.pallas_call(
        _topk_kernel,
        out_shape=jax.ShapeDtypeStruct((rows, LANES), I32),
        compiler_params=pltpu.CompilerParams(vmem_limit_bytes=VMEM_LIMIT_BYTES),
        name="select_topk",
    )(scores.reshape(rows, LANES))


def _sample_attn_kernel(sel_ref, pt_ref, q_ref, ks_ref, vs_ref, ck_ref, cv_ref, o_ref, kbuf, vbuf, sems, *, n_pages):
    b = pl.program_id(0)
    hd = A_HEAD_DIM
    slot = b % 2

    def slab_copies(page, head, dst_slot, j):
        return (pltpu.make_async_copy(ck_ref.at[page, head], kbuf.at[dst_slot, head, j], sems.at[dst_slot, 0]),
                pltpu.make_async_copy(cv_ref.at[page, head], vbuf.at[dst_slot, head, j], sems.at[dst_slot, 1]))

    def start_fetch(seq, dst_slot):
        for head in range(A_HEADS):
            for t in range(MOBA_TOPK):
                blk = sel_ref[(seq * A_HEADS + head) * MOBA_TOPK + t]
                for half in range(PAGES_PER_BLOCK):
                    page = pt_ref[seq * n_pages + blk * PAGES_PER_BLOCK + half]
                    for copy in slab_copies(page, head, dst_slot, t * PAGES_PER_BLOCK + half):
                        copy.start()

    @pl.when(b == 0)
    def _():
        start_fetch(0, 0)

    @pl.when(b + 1 < pl.num_programs(0))
    def _():
        start_fetch(b + 1, 1 - slot)

    for head in range(A_HEADS):
        for j in range(SAMPLE_SLABS):
            for copy in slab_copies(0, head, slot, j):
                copy.wait()

    nt = (((1,), (1,)), ((), ()))
    for pair in range(A_HEADS // 2):
        outs = []
        for hl in range(2):
            head = 2 * pair + hl
            cols = slice(head * hd, (head + 1) * hd)
            q = q_ref[0][:, cols] * (hd ** -0.5)
            q8 = jnp.broadcast_to(q, (SUBLANES, hd)).astype(BF16)
            s_self = jnp.sum(q * ks_ref[0][:, cols], axis=-1, keepdims=True)
            s = [jnp.dot(q8, kbuf[slot, head, j].astype(BF16), preferred_element_type=F32)
                 for j in range(SAMPLE_SLABS)]
            m = s_self
            for sp in s:
                m = jnp.maximum(m, jnp.max(sp, axis=-1, keepdims=True))
            p_self = jnp.exp(s_self - m)
            l = p_self
            acc = p_self * vs_ref[0][:, cols]
            for j, sp in enumerate(s):
                p = jnp.exp(sp - m)
                l = l + jnp.sum(p, axis=-1, keepdims=True)
                acc = acc + lax.dot_general(p.astype(BF16), vbuf[slot, head, j].astype(BF16), nt,
                                            preferred_element_type=F32)
            outs.append((acc / l)[0:1, :])
        o_ref[0, :, pair * 2 * hd:(pair + 1) * 2 * hd] = jnp.concatenate(outs, axis=1)


def _sample_attention(sel_flat, pt_flat, q_s, k_s, v_s, cache_kt, cache_vt, *, batch, n_pages):
    tok_spec = pl.BlockSpec((1, 1, A_WIDTH), lambda b, sel, pt: (b, 0, 0))
    slab_buf = pltpu.VMEM((2, A_HEADS, SAMPLE_SLABS, A_HEAD_DIM, PAGE_SIZE), F32)
    return pl.pallas_call(
        functools.partial(_sample_attn_kernel, n_pages=n_pages),
        grid_spec=pltpu.PrefetchScalarGridSpec(
            num_scalar_prefetch=2,
            grid=(batch,),
            in_specs=[tok_spec] * 3 + [pl.BlockSpec(memory_space=pl.ANY)] * 2,
            out_specs=tok_spec,
            scratch_shapes=[slab_buf, slab_buf, pltpu.SemaphoreType.DMA((2, 2))],
        ),
        out_shape=jax.ShapeDtypeStruct((batch, 1, A_WIDTH), F32),
        compiler_params=_params("arbitrary"),
        name="sample_attention",
    )(sel_flat, pt_flat, q_s.reshape(batch, 1, A_WIDTH), k_s.reshape(batch, 1, A_WIDTH),
      v_s.reshape(batch, 1, A_WIDTH), cache_kt, cache_vt)


def _sample_retention_kernel(q_ref, k_ref, v_ref, r_ref, y_ref, rnew_ref, *, decay):
    bb = q_ref.shape[0]
    kd, vd = R_KEY_DIM, R_VAL_DIM
    reps = LANES // bb
    qt = jnp.concatenate([q_ref[...]] * reps, axis=0).T
    kt = jnp.concatenate([k_ref[...]] * reps, axis=0).T
    for bi in range(bb):
        for h in range(R_HEADS):
            rows = slice(h * kd, (h + 1) * kd)
            qc = qt[rows, bi:bi + 1]
            kc = kt[rows, bi:bi + 1]
            vrow = v_ref[bi:bi + 1, h * vd:(h + 1) * vd]
            r = r_ref[bi, rows, :]
            qk = jnp.sum(qc * kc, axis=0, keepdims=True)
            cross = jnp.sum((qc * decay[h]) * r, axis=0, keepdims=True)
            o = qk * vrow + cross
            rnew_ref[bi, rows, :] = decay[h] * r + kc * vrow
            y = o * lax.rsqrt(jnp.mean(o * o, axis=-1, keepdims=True) + NORM_EPS)
            y_ref[bi:bi + 1, h * vd:(h + 1) * vd] = y


def _sample_retention(qr_s, kr_s, vr_s, state):
    batch = qr_s.shape[0]
    bb = SAMPLE_RET_BATCH
    lg32 = np.log(np.float32(1.0) - np.exp2(np.float32(-5.0) - np.arange(R_HEADS, dtype=np.float32)))
    decay = tuple(float(np.exp(lg32[h])) for h in range(R_HEADS))
    state3 = state.reshape(batch, R_HEADS * R_KEY_DIM, R_VAL_DIM)
    y, r_new = pl.pallas_call(
        functools.partial(_sample_retention_kernel, decay=decay),
        grid=(batch // bb,),
        in_specs=[
            pl.BlockSpec((bb, R_QK_WIDTH), lambda i: (i, 0)),
            pl.BlockSpec((bb, R_QK_WIDTH), lambda i: (i, 0)),
            pl.BlockSpec((bb, R_V_WIDTH), lambda i: (i, 0)),
            pl.BlockSpec((bb, R_HEADS * R_KEY_DIM, R_VAL_DIM), lambda i: (i, 0, 0)),
        ],
        out_specs=[
            pl.BlockSpec((bb, R_V_WIDTH), lambda i: (i, 0)),
            pl.BlockSpec((bb, R_HEADS * R_KEY_DIM, R_VAL_DIM), lambda i: (i, 0, 0)),
        ],
        out_shape=[
            jax.ShapeDtypeStruct((batch, R_V_WIDTH), F32),
            jax.ShapeDtypeStruct(state3.shape, F32),
        ],
        compiler_params=_params("arbitrary"),
        name="sample_retention",
    )(qr_s, kr_s, vr_s, state3)
    return y, r_new.reshape(state.shape)


def _layer(x_p, x_s, cache_k, cache_v, state_r, page_table, g_pre, w_in, b_merge, w_pa, w_pr, w_out, g_post):
    batch, seq, _ = x_p.shape
    dec_batch, dec_seq, _ = x_s.shape
    assert dec_seq == 1
    n_pages = page_table.shape[1]
    past_len = n_pages * PAGE_SIZE
    assert seq % PROMPT_TOKENS_PER_TILE == 0 and seq % RET_CHUNK == 0 and seq % (MOBA_GROUP * MOBA_BLOCK) == 0
    assert MOBA_TOPK <= n_pages // PAGES_PER_BLOCK <= LANES
    assert dec_batch % SAMPLE_RET_BATCH == 0

    o = _OFFS
    w_bf = w_in.astype(BF16)
    w_a = jnp.concatenate([w_bf[:, o[0]:o[3]], w_bf[:, o[4]:o[7]]], axis=1)
    w_g = jnp.concatenate([w_bf[:, o[3]:o[4]], w_bf[:, o[7]:o[8]]], axis=1)
    w_m = w_bf[:, o[8]:o[10]]
    w_pa_b, w_pr_b, w_out_b = w_pa.astype(BF16), w_pr.astype(BF16), w_out.astype(BF16)
    g_pre2, g_post2, b2 = g_pre.reshape(1, D_MODEL), g_post.reshape(1, D_MODEL), b_merge.reshape(1, 2 * D_MODEL)

    pos_p = jnp.arange(seq, dtype=I32)
    xp2 = x_p.reshape(batch * seq, D_MODEL)
    q, kf_t, kb, vf_t, vb_t, qr, kr, vr, km = _in_projection(
        xp2, g_pre2, w_a,
        _rope_tables(pos_p, ROPE_DIM, A_HEAD_DIM, ROPE_THETA),
        _rope_tables(pos_p, R_KEY_DIM, R_KEY_DIM, R_ROPE_THETA),
        tm=PROMPT_TOKENS_PER_TILE, prompt_seq=seq, act_dtype=BF16, q_scale=MOBA_Q_SCALE)
    km = km.reshape(batch * (seq // MOBA_BLOCK), A_WIDTH)
    pos_s = jnp.full((dec_batch,), past_len, dtype=I32)
    xs2 = x_s.reshape(dec_batch, D_MODEL)
    q_s, k_s, _, v_s, _, qr_s, kr_s, vr_s = _in_projection(
        xs2, g_pre2, w_a,
        _rope_tables(pos_s, ROPE_DIM, A_HEAD_DIM, ROPE_THETA),
        _rope_tables(pos_s, R_KEY_DIM, R_KEY_DIM, R_ROPE_THETA),
        tm=dec_batch, prompt_seq=None, act_dtype=F32, q_scale=None)
    cache_kt = cache_k.transpose(0, 2, 3, 1)
    cache_vt = cache_v.transpose(0, 2, 3, 1)
    pt_flat = page_table.reshape(-1)

    ya, block_scores = _moba_prompt_and_sample_select(q, kb, vb_t, km, q_s, cache_kt, pt_flat, batch=batch, seq=seq,
                                                      dec_batch=dec_batch, n_pages=n_pages)
    yr, r_p = _retention_prompt(qr, kr, vr, batch=batch, seq=seq)
    y_p = _merge(xp2, ya, yr, g_pre2, w_g, w_m, b2, w_pa_b, w_pr_b, w_out_b, g_post2, tm=PROMPT_TOKENS_PER_TILE)

    sel_flat = _select_topk(block_scores)[:, :MOBA_TOPK].reshape(-1)
    ya_s = _sample_attention(sel_flat, pt_flat, q_s, k_s, v_s, cache_kt, cache_vt,
                             batch=dec_batch, n_pages=n_pages).reshape(dec_batch, A_WIDTH)
    yr_s, r_s = _sample_retention(qr_s, kr_s, vr_s, state_r)
    y_s = _merge(xs2, ya_s, yr_s, g_pre2, w_g, w_m, b2, w_pa_b, w_pr_b, w_out_b, g_post2, tm=dec_batch)

    per_token = lambda a: a.reshape(batch, A_HEADS, A_HEAD_DIM, seq).transpose(0, 3, 1, 2)
    return (y_p.reshape(x_p.shape), y_s.reshape(x_s.shape), per_token(kf_t), per_token(vf_t), r_p,
            k_s.reshape(dec_batch, 1, A_HEADS, A_HEAD_DIM), v_s.reshape(dec_batch, 1, A_HEADS, A_HEAD_DIM), r_s)


def kernel(x_prompt, x_sample, cache_k, cache_v, state_ret, page_table, norm_pre_g, w_in, b_merge, w_proj_a,
           w_proj_r, w_out, norm_post_g):
    h_p, h_s = x_prompt, x_sample
    outs = [[] for _ in range(6)]
    for layer in range(w_in.shape[0]):
        res = _layer(h_p, h_s, cache_k[layer], cache_v[layer], state_ret[layer], page_table,
                     norm_pre_g[layer], w_in[layer], b_merge[layer], w_proj_a[layer], w_proj_r[layer],
                     w_out[layer], norm_post_g[layer])
        h_p, h_s = res[0], res[1]
        for acc, leaf in zip(outs, res[2:]):
            acc.append(leaf)
    return (h_p, h_s) + tuple(jnp.stack(o) for o in outs)
```

```python
import functools

import numpy as np
import jax
import jax.numpy as jnp
from jax import lax
from jax.experimental import pallas as pl
from jax.experimental.pallas import tpu as pltpu

F32 = jnp.float32
BF16 = jnp.bfloat16
I32 = jnp.int32

D_MODEL = 1024
A_HEADS = 8
A_HEAD_DIM = 64
A_WIDTH = A_HEADS * A_HEAD_DIM
MOBA_BLOCK = 256
MOBA_TOPK = 3
ROPE_THETA = 500000.0
ROPE_DIM = A_HEAD_DIM // 4
R_HEADS = 4
R_KEY_DIM = 64
R_VAL_DIM = 128
R_QK_WIDTH = R_HEADS * R_KEY_DIM
R_V_WIDTH = R_HEADS * R_VAL_DIM
R_ROPE_THETA = 10000.0
PAGE_SIZE = 128
NORM_EPS = 1e-6
NEG = -1e30

_OFFS = np.cumsum((0, A_WIDTH, A_WIDTH, A_WIDTH, A_WIDTH, R_QK_WIDTH, R_QK_WIDTH, R_V_WIDTH, R_V_WIDTH, D_MODEL, D_MODEL))

LANES = 128
SUBLANES = 8
VMEM_LIMIT_BYTES = 56 * 1024 * 1024

MOBA_Q_SCALE = A_HEAD_DIM ** -0.5 * float(np.log2(np.e))
MOBA_GROUP = 4
BF16_SUBLANES = 16
MOBA_V_ROWS = A_HEAD_DIM + BF16_SUBLANES
RET_CHUNK = 256
PROMPT_TOKENS_PER_TILE = 512
PAGES_PER_BLOCK = MOBA_BLOCK // PAGE_SIZE
SAMPLE_SLABS = MOBA_TOPK * PAGES_PER_BLOCK
SAMPLE_RET_BATCH = 16


def _params(*semantics):
    return pltpu.CompilerParams(dimension_semantics=semantics, vmem_limit_bytes=VMEM_LIMIT_BYTES)


def _rms_normed(x, g):
    return x * lax.rsqrt(jnp.mean(x * x, axis=-1, keepdims=True) + NORM_EPS) * g


def _rope_tables(pos, rot_dim, head_dim, theta):
    half = rot_dim // 2
    inv_freq = theta ** (-jnp.arange(half, dtype=F32) * (2.0 / rot_dim))
    ang = pos.astype(F32)[:, None] * inv_freq[None, :]
    cos, sin = jnp.cos(ang), jnp.sin(ang)
    p = pos.shape[0]
    rest = head_dim - rot_dim
    zh = jnp.zeros((p, half), F32)
    c = jnp.concatenate([cos, cos, jnp.ones((p, rest), F32)], axis=1)
    s1 = jnp.concatenate([-sin, zh, jnp.zeros((p, rest), F32)], axis=1)
    s2 = jnp.concatenate([zh, sin, jnp.zeros((p, rest), F32)], axis=1)
    reps = LANES // head_dim
    return tuple(jnp.tile(t, (1, reps)) for t in (c, s1, s2))


def _retention_log_decay():
    return jnp.log(1.0 - jnp.exp2(-5.0 - jnp.arange(R_HEADS, dtype=F32)))


def _rope_chunks(z, c_ref, s1_ref, s2_ref, half):
    c, s1, s2 = c_ref[...], s1_ref[...], s2_ref[...]
    out = []
    for j in range(z.shape[1] // LANES):
        zs = z[:, j * LANES:(j + 1) * LANES]
        up = pltpu.roll(zs, LANES - half, 1)
        down = pltpu.roll(zs, half, 1)
        out.append(zs * c + up * s1 + down * s2)
    return out


def _store_chunks(ref, chunks, scale=None):
    for j, ch in enumerate(chunks):
        if scale is not None:
            ch = ch * scale
        ref[:, j * LANES:(j + 1) * LANES] = ch.astype(ref.dtype)


def _store_chunks_token_minor(refs, chunks):
    for j, ch in enumerate(chunks):
        ct = ch.T
        for ref in refs:
            ref[0, j * LANES:(j + 1) * LANES, :] = ct.astype(ref.dtype)


def _inproj_kernel(x_ref, g_ref, w_ref, ca_ref, sa1_ref, sa2_ref, cr_ref, sr1_ref, sr2_ref,
                   q_ref, kf_ref, kb_ref, vf_ref, vb_ref, qr_ref, kr_ref, vr_ref, *km_refs, q_scale, token_minor_kv):
    h = _rms_normed(x_ref[...], g_ref[...]).astype(BF16)

    def proj(lo, hi):
        return jnp.dot(h, w_ref[:, lo:hi], preferred_element_type=F32)

    a, rq, rv = A_WIDTH, R_QK_WIDTH, R_V_WIDTH
    _store_chunks(q_ref, _rope_chunks(proj(0, a), ca_ref, sa1_ref, sa2_ref, ROPE_DIM // 2), scale=q_scale)
    k_chunks = _rope_chunks(proj(a, 2 * a), ca_ref, sa1_ref, sa2_ref, ROPE_DIM // 2)
    if token_minor_kv:
        _store_chunks_token_minor([kf_ref], k_chunks)
    else:
        _store_chunks(kf_ref, k_chunks)
    _store_chunks(kb_ref, k_chunks)
    if km_refs:
        (km_ref,) = km_refs
        for j, ch in enumerate(k_chunks):
            for blk in range(ch.shape[0] // MOBA_BLOCK):
                rows = ch[blk * MOBA_BLOCK:(blk + 1) * MOBA_BLOCK, :]
                km_ref[0, blk:blk + 1, j * LANES:(j + 1) * LANES] = (
                    jnp.sum(rows, axis=0, keepdims=True) * (1.0 / MOBA_BLOCK))
    v = proj(2 * a, 3 * a)
    if token_minor_kv:
        _store_chunks_token_minor([vf_ref, vb_ref], [v[:, j * LANES:(j + 1) * LANES] for j in range(a // LANES)])
    else:
        vf_ref[...] = v
        vb_ref[...] = v.astype(vb_ref.dtype)
    o = 3 * a
    _store_chunks(qr_ref, _rope_chunks(proj(o, o + rq), cr_ref, sr1_ref, sr2_ref, R_KEY_DIM // 2))
    _store_chunks(kr_ref, _rope_chunks(proj(o + rq, o + 2 * rq), cr_ref, sr1_ref, sr2_ref, R_KEY_DIM // 2),
                  scale=R_KEY_DIM ** -0.5)
    vr_ref[...] = proj(o + 2 * rq, o + 2 * rq + rv).astype(vr_ref.dtype)


def _in_projection(x2d, g_pre, w_a, tabs_a, tabs_r, *, tm, prompt_seq, act_dtype, q_scale):
    t = x2d.shape[0]
    n = t // tm
    tab_blocks = tabs_a[0].shape[0] // tm
    tab_spec = pl.BlockSpec((tm, LANES), lambda i: (i % tab_blocks, 0))
    row = lambda width: pl.BlockSpec((tm, width), lambda i: (i, 0))
    const = lambda shape: pl.BlockSpec(shape, lambda i: (0,) * len(shape))
    if prompt_seq is None:
        kv_shape, kv_spec = (t, A_WIDTH), row(A_WIDTH)
    else:
        seq_blocks = prompt_seq // tm
        kv_shape = (t // prompt_seq, A_WIDTH, prompt_seq)
        kv_spec = pl.BlockSpec((1, A_WIDTH, tm), lambda i: (i // seq_blocks, 0, i % seq_blocks))
    out_shape = [
        jax.ShapeDtypeStruct((t, A_WIDTH), act_dtype),
        jax.ShapeDtypeStruct(kv_shape, F32),
        jax.ShapeDtypeStruct((t, A_WIDTH), BF16),
        jax.ShapeDtypeStruct(kv_shape, F32),
        jax.ShapeDtypeStruct(kv_shape, BF16),
        jax.ShapeDtypeStruct((t, R_QK_WIDTH), act_dtype),
        jax.ShapeDtypeStruct((t, R_QK_WIDTH), act_dtype),
        jax.ShapeDtypeStruct((t, R_V_WIDTH), act_dtype),
    ]
    out_specs = [row(A_WIDTH), kv_spec, row(A_WIDTH), kv_spec, kv_spec] + [row(R_QK_WIDTH)] * 2 + [row(R_V_WIDTH)]
    if prompt_seq is not None:
        nb = tm // MOBA_BLOCK
        out_shape.append(jax.ShapeDtypeStruct((n, nb, A_WIDTH), F32))
        out_specs.append(pl.BlockSpec((1, nb, A_WIDTH), lambda i: (i, 0, 0)))
    return pl.pallas_call(
        functools.partial(_inproj_kernel, q_scale=q_scale, token_minor_kv=prompt_seq is not None),
        grid=(n,),
        in_specs=[row(D_MODEL), const((1, D_MODEL)), const(w_a.shape)] + [tab_spec] * 6,
        out_specs=out_specs,
        out_shape=out_shape,
        compiler_params=_params("arbitrary"),
        name="in_projection",
    )(x2d, g_pre, w_a, *tabs_a, *tabs_r)


def _moba_prompt_kernel(pt_ref, q_ref, k_ref, v_ref, km_ref, qsel_ref, ck_ref, o_ref, score_ref,
                        vt_ref, qs_ref, bias_ref, acc_ref, m_ref, qcol_ref, part_ref, page_buf, page_sem,
                        s0_ref, s1_ref, s2_ref, s3_ref, *, sel_steps, sel_blocks):
    i = pl.program_id(2)
    nb = MOBA_BLOCK
    hd = A_HEAD_DIM
    n_blocks = k_ref.shape[0] // nb
    n_sel_pages = page_buf.shape[1]

    step = (pl.program_id(0) * pl.num_programs(1) + pl.program_id(1)) * pl.num_programs(2) + i
    n_steps = pl.num_programs(0) * pl.num_programs(1) * pl.num_programs(2)
    slot = step % 2
    sel_c = step % sel_steps

    def page_copy(page, dst_slot, r):
        return pltpu.make_async_copy(ck_ref.at[page], page_buf.at[dst_slot, r], page_sem.at[dst_slot])

    def start_pages(s, dst_slot):
        for r in range(n_sel_pages):
            page_copy(pt_ref[s * n_sel_pages + r], dst_slot, r).start()

    @pl.when(step == 0)
    def _():
        start_pages(0, 0)

    @pl.when(step + 1 < n_steps)
    def _():
        start_pages(step + 1, 1 - slot)

    @pl.when(sel_c == 0)
    def _():
        _sample_scores_init(qsel_ref, qcol_ref, part_ref)

    @pl.when(i == 0)
    def _():
        ones_rows = (lax.broadcasted_iota(I32, (MOBA_V_ROWS - hd, nb), 0) == 0).astype(BF16)
        for c in range(n_blocks):
            for h in range(2):
                vt_ref[c, h, 0:hd, :] = v_ref[0, h * hd:(h + 1) * hd, c * nb:(c + 1) * nb]
                vt_ref[c, h, hd:MOBA_V_ROWS, :] = ones_rows

    qt = q_ref[...].astype(F32).T
    row = lax.broadcasted_iota(I32, qt.shape, 0)
    km = km_ref[...]
    km_hi = km.astype(BF16)
    km_lo = (km - km_hi.astype(F32)).astype(BF16)
    jrow = lax.broadcasted_iota(I32, (n_blocks, nb), 0)
    for h in range(2):
        qb = jnp.where((row >= hd) if h else (row < hd), qt, 0.0).astype(BF16)
        qs_ref[h] = qb
        sc = (jnp.dot(km_hi, qb, preferred_element_type=F32)
              + jnp.dot(km_lo, qb, preferred_element_type=F32))
        sc = jnp.where(jrow < i, sc, NEG)
        cnt = jnp.zeros(sc.shape, I32)
        for jp in range(n_blocks):
            r = sc[jp:jp + 1, :]
            beats = (r > sc) | ((r == sc) & (jp < jrow))
            cnt = cnt + beats.astype(I32)
        sel = (jrow < i) & (cnt < MOBA_TOPK)
        bias = jnp.where(sel, 0.0, NEG)
        for jp in range(n_blocks):
            bias_ref[jp + 1, h:h + 1, :] = bias[jp:jp + 1, :]
    bias_ref[0, 0:2, :] = jnp.zeros((2, nb), F32)
    m_ref[0:2, :] = jnp.full((2, nb), NEG, F32)
    acc_ref[...] = jnp.zeros(acc_ref.shape, F32)

    key_idx = lax.broadcasted_iota(I32, (nb, nb), 0)
    qry_idx = lax.broadcasted_iota(I32, (nb, nb), 1)

    def scores_into(s_ref, blk, causal=False):
        kj = k_ref[pl.ds(pl.multiple_of(blk * nb, nb), nb), :]
        for h in range(2):
            s = jnp.dot(kj, qs_ref[h], preferred_element_type=F32)
            s_ref[h] = jnp.where(key_idx <= qry_idx, s, NEG) if causal else s

    def attend(s_ref, t, v_blk):
        quarter = nb // 4
        for h in range(2):
            brow = bias_ref[t, h:h + 1, :]
            parts = [s_ref[h, r * quarter:(r + 1) * quarter, :] for r in range(4)]
            tall = jnp.maximum(jnp.maximum(parts[0], parts[1]), jnp.maximum(parts[2], parts[3]))
            m_old = m_ref[h:h + 1, :]
            m_new = jnp.maximum(m_old, jnp.max(tall, axis=0, keepdims=True) + brow)
            alpha = jnp.exp2(m_old - m_new)
            offset = jnp.where(brow < 0.0, -NEG, m_new)
            p = jnp.exp2(s_ref[h] - offset).astype(BF16)
            pv = jnp.dot(vt_ref[v_blk, h], p, preferred_element_type=F32)
            acc_ref[h] = alpha * acc_ref[h] + pv
            m_ref[h:h + 1, :] = m_new

    bufs = (s0_ref, s1_ref, s2_ref, s3_ref)

    for r in range(n_sel_pages):
        page_copy(0, slot, r).wait()
    _sample_scores_fold(sel_c, page_buf.at[slot], qcol_ref, part_ref)

    scores_into(bufs[0], i, causal=True)
    for r in range(1, MOBA_GROUP):
        scores_into(bufs[r], r - 1)

    attend(bufs[0], 0, i)
    for r in range(1, MOBA_GROUP):
        attend(bufs[r], r, r - 1)

    def group_body(g, carry):
        base = MOBA_GROUP * g - 1
        for r in range(MOBA_GROUP):
            scores_into(bufs[r], base + r)
        for r in range(MOBA_GROUP):
            attend(bufs[r], base + r + 1, base + r)
        return carry

    lax.fori_loop(1, i // MOBA_GROUP + 1, group_body, 0)

    outs = [acc_ref[h, 0:hd, :] / acc_ref[h, hd:hd + 1, :] for h in range(2)]
    o_ref[...] = jnp.concatenate(outs, axis=0).T.astype(o_ref.dtype)

    @pl.when(sel_c == sel_steps - 1)
    def _():
        _sample_scores_emit(part_ref, score_ref, sel_blocks)


def _moba_prompt_and_sample_select(q, k, v, km, q_s, cache_kt, pt_flat, *, batch, seq, dec_batch, n_pages):
    nb = MOBA_BLOCK
    n_blocks = seq // nb
    pairs = A_HEADS // 2
    pw = 2 * A_HEAD_DIM
    n_steps = batch * pairs * n_blocks
    pages_per_step = dec_batch * n_pages // n_steps
    assert pages_per_step * n_steps == dec_batch * n_pages, "sample key pages must split evenly over the grid"
    assert pages_per_step % PAGES_PER_BLOCK == 0 and n_pages % pages_per_step == 0
    sel_steps = n_pages // pages_per_step

    def step_of(b, p, i):
        return (b * pairs + p) * n_blocks + i

    return pl.pallas_call(
        functools.partial(_moba_prompt_kernel, sel_steps=sel_steps, sel_blocks=n_pages // PAGES_PER_BLOCK),
        grid_spec=pltpu.PrefetchScalarGridSpec(
            num_scalar_prefetch=1,
            grid=(batch, pairs, n_blocks),
            in_specs=[
                pl.BlockSpec((nb, pw), lambda b, p, i, pt: (b * n_blocks + i, p)),
                pl.BlockSpec((seq, pw), lambda b, p, i, pt: (b, p)),
                pl.BlockSpec((1, pw, seq), lambda b, p, i, pt: (b, p, 0)),
                pl.BlockSpec((n_blocks, pw), lambda b, p, i, pt: (b, p)),
                pl.BlockSpec((1, 1, A_WIDTH), lambda b, p, i, pt: (step_of(b, p, i) // sel_steps, 0, 0)),
                pl.BlockSpec(memory_space=pl.ANY),
            ],
            out_specs=[
                pl.BlockSpec((nb, pw), lambda b, p, i, pt: (b * n_blocks + i, p)),
                pl.BlockSpec((1, A_HEADS, LANES), lambda b, p, i, pt: (step_of(b, p, i) // sel_steps, 0, 0)),
            ],
            scratch_shapes=[
                pltpu.VMEM((n_blocks, 2, MOBA_V_ROWS, nb), BF16),
                pltpu.VMEM((2, pw, nb), BF16),
                pltpu.VMEM((n_blocks + 1, SUBLANES, nb), F32),
                pltpu.VMEM((2, MOBA_V_ROWS, nb), F32),
                pltpu.VMEM((SUBLANES, nb), F32),
                pltpu.VMEM((A_WIDTH, PAGE_SIZE), F32),
                pltpu.VMEM((A_HEADS * SUBLANES, LANES), F32),
                pltpu.VMEM((2, pages_per_step, A_HEADS, A_HEAD_DIM, PAGE_SIZE), F32),
                pltpu.SemaphoreType.DMA((2,)),
            ] + [pltpu.VMEM((2, nb, nb), F32)] * MOBA_GROUP,
        ),
        out_shape=[
            jax.ShapeDtypeStruct((batch * seq, A_WIDTH), BF16),
            jax.ShapeDtypeStruct((dec_batch, A_HEADS, LANES), F32),
        ],
        compiler_params=_params("arbitrary", "arbitrary", "arbitrary"),
        name="moba_prompt",
    )(pt_flat, q, k, v, km, q_s.reshape(dec_batch, 1, A_WIDTH), cache_kt)


def _retention_prompt_kernel(q_ref, k_ref, v_ref, dmat_ref, qdec_ref, kdec_ref, y_ref, rfin_ref, r_ref,
                             *, state_decay):
    c = pl.program_id(1)
    kd, vd = R_KEY_DIM, R_VAL_DIM

    @pl.when(c == 0)
    def _():
        r_ref[...] = jnp.zeros(r_ref.shape, F32)

    chunk = q_ref.shape[0]
    lane = lax.broadcasted_iota(I32, (chunk, 2 * kd), 1)
    for p in range(R_HEADS // 2):
        q2 = q_ref[:, p * 2 * kd:(p + 1) * 2 * kd].astype(F32)
        k2 = k_ref[:, p * 2 * kd:(p + 1) * 2 * kd]
        qd2 = q2 * qdec_ref[p]
        kdt = (k2.astype(F32) * kdec_ref[p]).T.astype(BF16)
        r2 = r_ref[p]
        r2b = r2.astype(BF16)
        for hl in range(2):
            h = 2 * p + hl
            own = (lane >= kd) == bool(hl)
            qm = jnp.where(own, q2, 0.0).astype(BF16)
            qdm = jnp.where(own, qd2, 0.0).astype(BF16)
            sc = lax.dot_general(qm, k2, (((1,), (1,)), ((), ())), preferred_element_type=F32) * dmat_ref[h]
            vh = v_ref[:, h * vd:(h + 1) * vd]
            o = (jnp.dot(sc.astype(BF16), vh, preferred_element_type=F32)
                 + jnp.dot(qdm, r2b, preferred_element_type=F32))
            sl = slice(hl * kd, (hl + 1) * kd)
            u = jnp.dot(kdt[sl, :], vh, preferred_element_type=F32)
            r_ref[p, sl, :] = state_decay[h] * r2[sl, :] + u
            y = o * lax.rsqrt(jnp.mean(o * o, axis=-1, keepdims=True) + NORM_EPS)
            y_ref[:, h * vd:(h + 1) * vd] = y.astype(y_ref.dtype)

    @pl.when(c == pl.num_programs(1) - 1)
    def _():
        for h in range(R_HEADS):
            rfin_ref[0, h] = r_ref[h // 2, (h % 2) * kd:(h % 2 + 1) * kd, :]


def _retention_prompt(qr, kr, vr, *, batch, seq):
    chunk = RET_CHUNK
    nc = seq // chunk
    lg = _retention_log_decay()
    idx = jnp.arange(chunk, dtype=F32)
    diff = idx[:, None] - idx[None, :]
    dmat = jnp.where(diff >= 0, jnp.exp(lg[:, None, None] * jnp.maximum(diff, 0.0)), 0.0)
    qdec = jnp.exp(lg[:, None] * (idx + 1.0))
    kdec = jnp.exp(lg[:, None] * (chunk - 1.0 - idx))
    pair_table = lambda t: jnp.repeat(t.reshape(R_HEADS // 2, 2, chunk), R_KEY_DIM, axis=1).transpose(0, 2, 1)
    lg32 = np.log(np.float32(1.0) - np.exp2(np.float32(-5.0) - np.arange(R_HEADS, dtype=np.float32)))
    state_decay = tuple(float(np.exp(lg32[h] * np.float32(chunk))) for h in range(R_HEADS))
    const = lambda shape: pl.BlockSpec(shape, lambda b, c: (0,) * len(shape))
    return pl.pallas_call(
        functools.partial(_retention_prompt_kernel, state_decay=state_decay),
        grid=(batch, nc),
        in_specs=[
            pl.BlockSpec((chunk, R_QK_WIDTH), lambda b, c: (b * nc + c, 0)),
            pl.BlockSpec((chunk, R_QK_WIDTH), lambda b, c: (b * nc + c, 0)),
            pl.BlockSpec((chunk, R_V_WIDTH), lambda b, c: (b * nc + c, 0)),
            const((R_HEADS, chunk, chunk)),
            const((R_HEADS // 2, chunk, 2 * R_KEY_DIM)),
            const((R_HEADS // 2, chunk, 2 * R_KEY_DIM)),
        ],
        out_specs=[
            pl.BlockSpec((chunk, R_V_WIDTH), lambda b, c: (b * nc + c, 0)),
            pl.BlockSpec((1, R_HEADS, R_KEY_DIM, R_VAL_DIM), lambda b, c: (b, 0, 0, 0)),
        ],
        out_shape=[
            jax.ShapeDtypeStruct((batch * seq, R_V_WIDTH), BF16),
            jax.ShapeDtypeStruct((batch, R_HEADS, R_KEY_DIM, R_VAL_DIM), F32),
        ],
        scratch_shapes=[pltpu.VMEM((R_HEADS // 2, 2 * R_KEY_DIM, R_VAL_DIM), F32)],
        compiler_params=_params("arbitrary", "arbitrary"),
        name="retention_prompt",
    )(qr, kr, vr, dmat, pair_table(qdec), pair_table(kdec))


def _merge_kernel(x_ref, ya_ref, yr_ref, g_ref, wg_ref, wm_ref, b_ref, wpa_ref, wpr_ref, wo_ref, gp_ref, y_ref):
    x = x_ref[...]
    h = _rms_normed(x, g_ref[...]).astype(BF16)
    gates = jnp.dot(h, wg_ref[...], preferred_element_type=F32)
    ua = (ya_ref[...].astype(F32) * jax.nn.silu(gates[:, :A_WIDTH])).astype(BF16)
    ur = (yr_ref[...].astype(F32) * jax.nn.silu(gates[:, A_WIDTH:])).astype(BF16)
    pa = jnp.dot(ua, wpa_ref[...], preferred_element_type=F32)
    pr = jnp.dot(ur, wpr_ref[...], preferred_element_type=F32)
    mix = jnp.dot(h, wm_ref[...], preferred_element_type=F32) + b_ref[...]
    m = jax.nn.sigmoid(mix[:, :D_MODEL]) * pa + jax.nn.sigmoid(mix[:, D_MODEL:]) * pr
    o = jnp.dot(m.astype(BF16), wo_ref[...], preferred_element_type=F32)
    y_ref[...] = x + _rms_normed(o, gp_ref[...])


def _merge(x2d, ya, yr, g_pre, w_g, w_m, b_merge, w_pa, w_pr, w_out, g_post, *, tm):
    t = x2d.shape[0]
    row = lambda width: pl.BlockSpec((tm, width), lambda i: (i, 0))
    const = lambda a: pl.BlockSpec(a.shape, lambda i: (0,) * a.ndim)
    return pl.pallas_call(
        _merge_kernel,
        grid=(t // tm,),
        in_specs=[row(D_MODEL), row(A_WIDTH), row(R_V_WIDTH), const(g_pre), const(w_g), const(w_m),
                  const(b_merge), const(w_pa), const(w_pr), const(w_out), const(g_post)],
        out_specs=row(D_MODEL),
        out_shape=jax.ShapeDtypeStruct((t, D_MODEL), F32),
        compiler_params=_params("arbitrary"),
        name="gate_merge",
    )(x2d, ya, yr, g_pre, w_g, w_m, b_merge, w_pa, w_pr, w_out, g_post)


def _sample_scores_init(q_ref, qcol_ref, part_ref):
    part_ref[...] = jnp.zeros(part_ref.shape, F32)
    eye = lax.broadcasted_iota(I32, (LANES, LANES), 0) == lax.broadcasted_iota(I32, (LANES, LANES), 1)
    for j in range(A_WIDTH // LANES):
        qrow = jnp.broadcast_to(q_ref[0][:, j * LANES:(j + 1) * LANES], (LANES, LANES))
        col = jnp.sum(jnp.where(eye, qrow, 0.0), axis=1, keepdims=True)
        qcol_ref[j * LANES:(j + 1) * LANES, :] = jnp.broadcast_to(col, (LANES, PAGE_SIZE))


def _sample_scores_fold(c, pages_ref, qcol_ref, part_ref):
    blocks_per_step = pages_ref.shape[0] // PAGES_PER_BLOCK
    groups = A_HEAD_DIM // SUBLANES
    folded = [[None] * A_HEADS for _ in range(blocks_per_step)]
    for h in range(A_HEADS):
        qh = qcol_ref[h * A_HEAD_DIM:(h + 1) * A_HEAD_DIM, :]
        for r in range(blocks_per_step):
            tot = pages_ref[r * PAGES_PER_BLOCK, h]
            for half in range(1, PAGES_PER_BLOCK):
                tot = tot + pages_ref[r * PAGES_PER_BLOCK + half, h]
            folded[r][h] = jnp.sum((tot * qh).reshape(groups, SUBLANES, PAGE_SIZE), axis=0)
    lane = lax.broadcasted_iota(I32, part_ref.shape, 1)
    part = part_ref[...]
    for r in range(blocks_per_step):
        col = jnp.sum(jnp.concatenate(folded[r], axis=0), axis=1, keepdims=True)
        part = jnp.where(lane == c * blocks_per_step + r, col, part)
    part_ref[...] = part


def _sample_scores_emit(part_ref, score_ref, n_blocks):
    sc = jnp.sum(part_ref[...].reshape(A_HEADS, SUBLANES, LANES), axis=1) * (1.0 / MOBA_BLOCK)
    blk = lax.broadcasted_iota(I32, sc.shape, 1)
    score_ref[0] = jnp.where(blk < n_blocks, sc, -jnp.inf)


def _topk_kernel(score_ref, idx_ref):
    sc = score_ref[...]
    blk = lax.broadcasted_iota(I32, sc.shape, 1)
    out = jnp.zeros(sc.shape, I32)
    for t in range(MOBA_TOPK):
        mx = jnp.max(sc, axis=1, keepdims=True)
        first = jnp.min(jnp.where(sc == mx, blk, LANES), axis=1, keepdims=True)
        out = jnp.where(blk == t, first, out)
        sc = jnp.where(blk == first, -jnp.inf, sc)
    idx_ref[...] = out


def _select_topk(scores):
    rows = scores.shape[0] * scores.shape[1]
    return pl.pallas_call(
        _topk_kernel,
        out_shape=jax.ShapeDtypeStruct((rows, LANES), I32),
        compiler_params=pltpu.CompilerParams(vmem_limit_bytes=VMEM_LIMIT_BYTES),
        name="select_topk",
    )(scores.reshape(rows, LANES))


def _sample_attn_kernel(sel_ref, pt_ref, q_ref, ks_ref, vs_ref, ck_ref, cv_ref, o_ref, kbuf, vbuf, sems, *, n_pages):
    b = pl.program_id(0)
    hd = A_HEAD_DIM
    slot = b % 2

    def slab_copies(page, head, dst_slot, j):
        return (pltpu.make_async_copy(ck_ref.at[page, head], kbuf.at[dst_slot, head, j], sems.at[dst_slot, 0]),
                pltpu.make_async_copy(cv_ref.at[page, head], vbuf.at[dst_slot, head, j], sems.at[dst_slot, 1]))

    def start_fetch(seq, dst_slot):
        for head in range(A_HEADS):
            for t in range(MOBA_TOPK):
                blk = sel_ref[(seq * A_HEADS + head) * MOBA_TOPK + t]
                for half in range(PAGES_PER_BLOCK):
                    page = pt_ref[seq * n_pages + blk * PAGES_PER_BLOCK + half]
                    for copy in slab_copies(page, head, dst_slot, t * PAGES_PER_BLOCK + half):
                        copy.start()

    @pl.when(b == 0)
    def _():
        start_fetch(0, 0)

    @pl.when(b + 1 < pl.num_programs(0))
    def _():
        start_fetch(b + 1, 1 - slot)

    for head in range(A_HEADS):
        for j in range(SAMPLE_SLABS):
            for copy in slab_copies(0, head, slot, j):
                copy.wait()

    nt = (((1,), (1,)), ((), ()))
    for pair in range(A_HEADS // 2):
        outs = []
        for hl in range(2):
            head = 2 * pair + hl
            cols = slice(head * hd, (head + 1) * hd)
            q = q_ref[0][:, cols] * (hd ** -0.5)
            q8 = jnp.broadcast_to(q, (SUBLANES, hd)).astype(BF16)
            s_self = jnp.sum(q * ks_ref[0][:, cols], axis=-1, keepdims=True)
            def block_slabs(buf, t):
                return jnp.concatenate([buf[slot, head, t * PAGES_PER_BLOCK + half].astype(BF16)
                                        for half in range(PAGES_PER_BLOCK)], axis=1)
            s = [jnp.dot(q8, block_slabs(kbuf, t), preferred_element_type=F32)
                 for t in range(MOBA_TOPK)]
            m = s_self
            for sp in s:
                m = jnp.maximum(m, jnp.max(sp, axis=-1, keepdims=True))
            p_self = jnp.exp(s_self - m)
            l = p_self
            acc = p_self * vs_ref[0][:, cols]
            for t, sp in enumerate(s):
                p = jnp.exp(sp - m)
                l = l + jnp.sum(p, axis=-1, keepdims=True)
                acc = acc + lax.dot_general(p.astype(BF16), block_slabs(vbuf, t), nt,
                                            preferred_element_type=F32)
            outs.append((acc / l)[0:1, :])
        o_ref[0, :, pair * 2 * hd:(pair + 1) * 2 * hd] = jnp.concatenate(outs, axis=1)


def _sample_attention(sel_flat, pt_flat, q_s, k_s, v_s, cache_kt, cache_vt, *, batch, n_pages):
    tok_spec = pl.BlockSpec((1, 1, A_WIDTH), lambda b, sel, pt: (b, 0, 0))
    slab_buf = pltpu.VMEM((2, A_HEADS, SAMPLE_SLABS, A_HEAD_DIM, PAGE_SIZE), F32)
    return pl.pallas_call(
        functools.partial(_sample_attn_kernel, n_pages=n_pages),
        grid_spec=pltpu.PrefetchScalarGridSpec(
            num_scalar_prefetch=2,
            grid=(batch,),
            in_specs=[tok_spec] * 3 + [pl.BlockSpec(memory_space=pl.ANY)] * 2,
            out_specs=tok_spec,
            scratch_shapes=[slab_buf, slab_buf, pltpu.SemaphoreType.DMA((2, 2))],
        ),
        out_shape=jax.ShapeDtypeStruct((batch, 1, A_WIDTH), F32),
        compiler_params=_params("arbitrary"),
        name="sample_attention",
    )(sel_flat, pt_flat, q_s.reshape(batch, 1, A_WIDTH), k_s.reshape(batch, 1, A_WIDTH),
      v_s.reshape(batch, 1, A_WIDTH), cache_kt, cache_vt)


def _sample_retention_kernel(q_ref, k_ref, v_ref, r_ref, y_ref, rnew_ref, *, decay):
    bb = q_ref.shape[0]
    kd, vd = R_KEY_DIM, R_VAL_DIM
    reps = LANES // bb
    qt = jnp.concatenate([q_ref[...]] * reps, axis=0).T
    kt = jnp.concatenate([k_ref[...]] * reps, axis=0).T
    for bi in range(bb):
        for h in range(R_HEADS):
            rows = slice(h * kd, (h + 1) * kd)
            qc = qt[rows, bi:bi + 1]
            kc = kt[rows, bi:bi + 1]
            vrow = v_ref[bi:bi + 1, h * vd:(h + 1) * vd]
            r = r_ref[bi, rows, :]
            qk = jnp.sum(qc * kc, axis=0, keepdims=True)
            cross = jnp.sum((qc * decay[h]) * r, axis=0, keepdims=True)
            o = qk * vrow + cross
            rnew_ref[bi, rows, :] = decay[h] * r + kc * vrow
            y = o * lax.rsqrt(jnp.mean(o * o, axis=-1, keepdims=True) + NORM_EPS)
            y_ref[bi:bi + 1, h * vd:(h + 1) * vd] = y


def _sample_retention(qr_s, kr_s, vr_s, state):
    batch = qr_s.shape[0]
    bb = SAMPLE_RET_BATCH
    lg32 = np.log(np.float32(1.0) - np.exp2(np.float32(-5.0) - np.arange(R_HEADS, dtype=np.float32)))
    decay = tuple(float(np.exp(lg32[h])) for h in range(R_HEADS))
    state3 = state.reshape(batch, R_HEADS * R_KEY_DIM, R_VAL_DIM)
    y, r_new = pl.pallas_call(
        functools.partial(_sample_retention_kernel, decay=decay),
        grid=(batch // bb,),
        in_specs=[
            pl.BlockSpec((bb, R_QK_WIDTH), lambda i: (i, 0)),
            pl.BlockSpec((bb, R_QK_WIDTH), lambda i: (i, 0)),
            pl.BlockSpec((bb, R_V_WIDTH), lambda i: (i, 0)),
            pl.BlockSpec((bb, R_HEADS * R_KEY_DIM, R_VAL_DIM), lambda i: (i, 0, 0)),
        ],
        out_specs=[
            pl.BlockSpec((bb, R_V_WIDTH), lambda i: (i, 0)),
            pl.BlockSpec((bb, R_HEADS * R_KEY_DIM, R_VAL_DIM), lambda i: (i, 0, 0)),
        ],
        out_shape=[
            jax.ShapeDtypeStruct((batch, R_V_WIDTH), F32),
            jax.ShapeDtypeStruct(state3.shape, F32),
        ],
        compiler_params=_params("arbitrary"),
        name="sample_retention",
    )(qr_s, kr_s, vr_s, state3)
    return y, r_new.reshape(state.shape)


def _layer(x_p, x_s, cache_k, cache_v, state_r, page_table, g_pre, w_in, b_merge, w_pa, w_pr, w_out, g_post):
    batch, seq, _ = x_p.shape
    dec_batch, dec_seq, _ = x_s.shape
    assert dec_seq == 1
    n_pages = page_table.shape[1]
    past_len = n_pages * PAGE_SIZE
    assert seq % PROMPT_TOKENS_PER_TILE == 0 and seq % RET_CHUNK == 0 and seq % (MOBA_GROUP * MOBA_BLOCK) == 0
    assert MOBA_TOPK <= n_pages // PAGES_PER_BLOCK <= LANES
    assert dec_batch % SAMPLE_RET_BATCH == 0

    o = _OFFS
    w_bf = w_in.astype(BF16)
    w_a = jnp.concatenate([w_bf[:, o[0]:o[3]], w_bf[:, o[4]:o[7]]], axis=1)
    w_g = jnp.concatenate([w_bf[:, o[3]:o[4]], w_bf[:, o[7]:o[8]]], axis=1)
    w_m = w_bf[:, o[8]:o[10]]
    w_pa_b, w_pr_b, w_out_b = w_pa.astype(BF16), w_pr.astype(BF16), w_out.astype(BF16)
    g_pre2, g_post2, b2 = g_pre.reshape(1, D_MODEL), g_post.reshape(1, D_MODEL), b_merge.reshape(1, 2 * D_MODEL)

    pos_p = jnp.arange(seq, dtype=I32)
    xp2 = x_p.reshape(batch * seq, D_MODEL)
    q, kf_t, kb, vf_t, vb_t, qr, kr, vr, km = _in_projection(
        xp2, g_pre2, w_a,
        _rope_tables(pos_p, ROPE_DIM, A_HEAD_DIM, ROPE_THETA),
        _rope_tables(pos_p, R_KEY_DIM, R_KEY_DIM, R_ROPE_THETA),
        tm=PROMPT_TOKENS_PER_TILE, prompt_seq=seq, act_dtype=BF16, q_scale=MOBA_Q_SCALE)
    km = km.reshape(batch * (seq // MOBA_BLOCK), A_WIDTH)
    pos_s = jnp.full((dec_batch,), past_len, dtype=I32)
    xs2 = x_s.reshape(dec_batch, D_MODEL)
    q_s, k_s, _, v_s, _, qr_s, kr_s, vr_s = _in_projection(
        xs2, g_pre2, w_a,
        _rope_tables(pos_s, ROPE_DIM, A_HEAD_DIM, ROPE_THETA),
        _rope_tables(pos_s, R_KEY_DIM, R_KEY_DIM, R_ROPE_THETA),
        tm=dec_batch, prompt_seq=None, act_dtype=F32, q_scale=None)
    cache_kt = cache_k.transpose(0, 2, 3, 1)
    cache_vt = cache_v.transpose(0, 2, 3, 1)
    pt_flat = page_table.reshape(-1)

    ya, block_scores = _moba_prompt_and_sample_select(q, kb, vb_t, km, q_s, cache_kt, pt_flat, batch=batch, seq=seq,
                                                      dec_batch=dec_batch, n_pages=n_pages)
    yr, r_p = _retention_prompt(qr, kr, vr, batch=batch, seq=seq)
    y_p = _merge(xp2, ya, yr, g_pre2, w_g, w_m, b2, w_pa_b, w_pr_b, w_out_b, g_post2, tm=PROMPT_TOKENS_PER_TILE)

    sel_flat = _select_topk(block_scores)[:, :MOBA_TOPK].reshape(-1)
    ya_s = _sample_attention(sel_flat, pt_flat, q_s, k_s, v_s, cache_kt, cache_vt,
                             batch=dec_batch, n_pages=n_pages).reshape(dec_batch, A_WIDTH)
    yr_s, r_s = _sample_retention(qr_s, kr_s, vr_s, state_r)
    y_s = _merge(xs2, ya_s, yr_s, g_pre2, w_g, w_m, b2, w_pa_b, w_pr_b, w_out_b, g_post2, tm=dec_batch)

    per_token = lambda a: a.reshape(batch, A_HEADS, A_HEAD_DIM, seq).transpose(0, 3, 1, 2)
    return (y_p.reshape(x_p.shape), y_s.reshape(x_s.shape), per_token(kf_t), per_token(vf_t), r_p,
            k_s.reshape(dec_batch, 1, A_HEADS, A_HEAD_DIM), v_s.reshape(dec_batch, 1, A_HEADS, A_HEAD_DIM), r_s)


def kernel(x_prompt, x_sample, cache_k, cache_v, state_ret, page_table, norm_pre_g, w_in, b_merge, w_proj_a,
           w_proj_r, w_out, norm_post_g):
    h_p, h_s = x_prompt, x_sample
    outs = [[] for _ in range(6)]
    for layer in range(w_in.shape[0]):
        res = _layer(h_p, h_s, cache_k[layer], cache_v[layer], state_ret[layer], page_table,
                     norm_pre_g[layer], w_in[layer], b_merge[layer], w_proj_a[layer], w_proj_r[layer],
                     w_out[layer], norm_post_g[layer])
        h_p, h_s = res[0], res[1]
        for acc, leaf in zip(outs, res[2:]):
            acc.append(leaf)
    return (h_p, h_s) + tuple(jnp.stack(o) for o in outs)
```

```python
import functools

import numpy as np
import jax
import jax.numpy as jnp
from jax import lax
from jax.experimental import pallas as pl
from jax.experimental.pallas import tpu as pltpu

F32 = jnp.float32
BF16 = jnp.bfloat16
I32 = jnp.int32

D_MODEL = 1024
A_HEADS = 8
A_HEAD_DIM = 64
A_WIDTH = A_HEADS * A_HEAD_DIM
MOBA_BLOCK = 256
MOBA_TOPK = 3
ROPE_THETA = 500000.0
ROPE_DIM = A_HEAD_DIM // 4
R_HEADS = 4
R_KEY_DIM = 64
R_VAL_DIM = 128
R_QK_WIDTH = R_HEADS * R_KEY_DIM
R_V_WIDTH = R_HEADS * R_VAL_DIM
R_ROPE_THETA = 10000.0
PAGE_SIZE = 128
NORM_EPS = 1e-6
NEG = -1e30

_OFFS = np.cumsum((0, A_WIDTH, A_WIDTH, A_WIDTH, A_WIDTH, R_QK_WIDTH, R_QK_WIDTH, R_V_WIDTH, R_V_WIDTH, D_MODEL, D_MODEL))

LANES = 128
SUBLANES = 8
VMEM_LIMIT_BYTES = 56 * 1024 * 1024

MOBA_Q_SCALE = A_HEAD_DIM ** -0.5 * float(np.log2(np.e))
MOBA_GROUP = 4
BF16_SUBLANES = 16
MOBA_V_ROWS = A_HEAD_DIM + BF16_SUBLANES
RET_CHUNK = 256
PROMPT_TOKENS_PER_TILE = 512
MERGE_SUB_ROWS = 256
PAGES_PER_BLOCK = MOBA_BLOCK // PAGE_SIZE
SAMPLE_SLABS = MOBA_TOPK * PAGES_PER_BLOCK
SAMPLE_RET_BATCH = 16


def _params(*semantics):
    return pltpu.CompilerParams(dimension_semantics=semantics, vmem_limit_bytes=VMEM_LIMIT_BYTES)


def _rms_normed(x, g):
    return x * lax.rsqrt(jnp.mean(x * x, axis=-1, keepdims=True) + NORM_EPS) * g


def _rope_tables(pos, rot_dim, head_dim, theta):
    half = rot_dim // 2
    inv_freq = theta ** (-jnp.arange(half, dtype=F32) * (2.0 / rot_dim))
    d = jnp.arange(LANES) % head_dim
    ang = pos.astype(F32)[:, None] * inv_freq[d % half][None, :]
    cos, sin = jnp.cos(ang), jnp.sin(ang)
    c = jnp.where(d < rot_dim, cos, 1.0)
    s1 = jnp.where(d < half, -sin, 0.0)
    s2 = jnp.where((d >= half) & (d < rot_dim), sin, 0.0)
    return c, s1, s2


def _retention_log_decay():
    return jnp.log(1.0 - jnp.exp2(-5.0 - jnp.arange(R_HEADS, dtype=F32)))


def _rope_chunks(z, c_ref, s1_ref, s2_ref, half):
    c, s1, s2 = c_ref[...], s1_ref[...], s2_ref[...]
    out = []
    for j in range(z.shape[1] // LANES):
        zs = z[:, j * LANES:(j + 1) * LANES]
        up = pltpu.roll(zs, LANES - half, 1)
        down = pltpu.roll(zs, half, 1)
        out.append(zs * c + up * s1 + down * s2)
    return out


def _store_chunks(ref, chunks, scale=None):
    for j, ch in enumerate(chunks):
        if scale is not None:
            ch = ch * scale
        ref[:, j * LANES:(j + 1) * LANES] = ch.astype(ref.dtype)


def _store_chunks_token_minor(refs, chunks):
    for j, ch in enumerate(chunks):
        ct = ch.T
        for ref in refs:
            ref[0, j * LANES:(j + 1) * LANES, :] = ct.astype(ref.dtype)


def _inproj_kernel(x_ref, g_ref, w_ref, ca_ref, sa1_ref, sa2_ref, cr_ref, sr1_ref, sr2_ref,
                   q_ref, kf_ref, kb_ref, vf_ref, vb_ref, qr_ref, kr_ref, vr_ref, *km_refs, q_scale, token_minor_kv):
    h = _rms_normed(x_ref[...], g_ref[...]).astype(BF16)

    def proj(lo, hi):
        return jnp.dot(h, w_ref[:, lo:hi], preferred_element_type=F32)

    a, rq, rv = A_WIDTH, R_QK_WIDTH, R_V_WIDTH
    _store_chunks(q_ref, _rope_chunks(proj(0, a), ca_ref, sa1_ref, sa2_ref, ROPE_DIM // 2), scale=q_scale)
    k_chunks = _rope_chunks(proj(a, 2 * a), ca_ref, sa1_ref, sa2_ref, ROPE_DIM // 2)
    if token_minor_kv:
        _store_chunks_token_minor([kf_ref], k_chunks)
    else:
        _store_chunks(kf_ref, k_chunks)
    _store_chunks(kb_ref, k_chunks)
    if km_refs:
        (km_ref,) = km_refs
        for j, ch in enumerate(k_chunks):
            for blk in range(ch.shape[0] // MOBA_BLOCK):
                rows = ch[blk * MOBA_BLOCK:(blk + 1) * MOBA_BLOCK, :]
                km_ref[0, blk:blk + 1, j * LANES:(j + 1) * LANES] = (
                    jnp.sum(rows, axis=0, keepdims=True) * (1.0 / MOBA_BLOCK))
    v = proj(2 * a, 3 * a)
    if token_minor_kv:
        _store_chunks_token_minor([vf_ref, vb_ref], [v[:, j * LANES:(j + 1) * LANES] for j in range(a // LANES)])
    else:
        vf_ref[...] = v
        vb_ref[...] = v.astype(vb_ref.dtype)
    o = 3 * a
    _store_chunks(qr_ref, _rope_chunks(proj(o, o + rq), cr_ref, sr1_ref, sr2_ref, R_KEY_DIM // 2))
    _store_chunks(kr_ref, _rope_chunks(proj(o + rq, o + 2 * rq), cr_ref, sr1_ref, sr2_ref, R_KEY_DIM // 2),
                  scale=R_KEY_DIM ** -0.5)
    vr_ref[...] = proj(o + 2 * rq, o + 2 * rq + rv).astype(vr_ref.dtype)


def _in_projection(x2d, g_pre, w_a, tabs_a, tabs_r, *, tm, prompt_seq, act_dtype, q_scale):
    t = x2d.shape[0]
    n = t // tm
    tab_blocks = tabs_a[0].shape[0] // tm
    tab_spec = pl.BlockSpec((tm, LANES), lambda i: (i % tab_blocks, 0))
    row = lambda width: pl.BlockSpec((tm, width), lambda i: (i, 0))
    const = lambda shape: pl.BlockSpec(shape, lambda i: (0,) * len(shape))
    if prompt_seq is None:
        kv_shape, kv_spec = (t, A_WIDTH), row(A_WIDTH)
    else:
        seq_blocks = prompt_seq // tm
        kv_shape = (t // prompt_seq, A_WIDTH, prompt_seq)
        kv_spec = pl.BlockSpec((1, A_WIDTH, tm), lambda i: (i // seq_blocks, 0, i % seq_blocks))
    out_shape = [
        jax.ShapeDtypeStruct((t, A_WIDTH), act_dtype),
        jax.ShapeDtypeStruct(kv_shape, F32),
        jax.ShapeDtypeStruct((t, A_WIDTH), BF16),
        jax.ShapeDtypeStruct(kv_shape, F32),
        jax.ShapeDtypeStruct(kv_shape, BF16),
        jax.ShapeDtypeStruct((t, R_QK_WIDTH), act_dtype),
        jax.ShapeDtypeStruct((t, R_QK_WIDTH), act_dtype),
        jax.ShapeDtypeStruct((t, R_V_WIDTH), act_dtype),
    ]
    out_specs = [row(A_WIDTH), kv_spec, row(A_WIDTH), kv_spec, kv_spec] + [row(R_QK_WIDTH)] * 2 + [row(R_V_WIDTH)]
    if prompt_seq is not None:
        nb = tm // MOBA_BLOCK
        out_shape.append(jax.ShapeDtypeStruct((n, nb, A_WIDTH), F32))
        out_specs.append(pl.BlockSpec((1, nb, A_WIDTH), lambda i: (i, 0, 0)))
    return pl.pallas_call(
        functools.partial(_inproj_kernel, q_scale=q_scale, token_minor_kv=prompt_seq is not None),
        grid=(n,),
        in_specs=[row(D_MODEL), const((1, D_MODEL)), const(w_a.shape)] + [tab_spec] * 6,
        out_specs=out_specs,
        out_shape=out_shape,
        compiler_params=_params("arbitrary"),
        name="in_projection",
    )(x2d, g_pre, w_a, *tabs_a, *tabs_r)


def _moba_prompt_kernel(pt_ref, q_ref, k_ref, v_ref, km_ref, qsel_ref, ck_ref, o_ref, score_ref,
                        vt_ref, qs_ref, bias_ref, acc_ref, m_ref, qcol_ref, part_ref, page_buf, page_sem,
                        s0_ref, s1_ref, s2_ref, s3_ref, *, sel_steps, sel_blocks):
    i = pl.program_id(2)
    nb = MOBA_BLOCK
    hd = A_HEAD_DIM
    n_blocks = k_ref.shape[0] // nb
    n_sel_pages = page_buf.shape[1]

    step = (pl.program_id(0) * pl.num_programs(1) + pl.program_id(1)) * pl.num_programs(2) + i
    n_steps = pl.num_programs(0) * pl.num_programs(1) * pl.num_programs(2)
    slot = step % 2
    sel_c = step % sel_steps

    def page_copy(page, dst_slot, r):
        return pltpu.make_async_copy(ck_ref.at[page], page_buf.at[dst_slot, r], page_sem.at[dst_slot])

    def start_pages(s, dst_slot):
        for r in range(n_sel_pages):
            page_copy(pt_ref[s * n_sel_pages + r], dst_slot, r).start()

    @pl.when(step == 0)
    def _():
        start_pages(0, 0)

    @pl.when(step + 1 < n_steps)
    def _():
        start_pages(step + 1, 1 - slot)

    @pl.when(sel_c == 0)
    def _():
        _sample_scores_init(qsel_ref, qcol_ref, part_ref)

    @pl.when(i == 0)
    def _():
        ones_rows = (lax.broadcasted_iota(I32, (MOBA_V_ROWS - hd, nb), 0) == 0).astype(BF16)
        for c in range(n_blocks):
            for h in range(2):
                vt_ref[c, h, 0:hd, :] = v_ref[0, h * hd:(h + 1) * hd, c * nb:(c + 1) * nb]
                vt_ref[c, h, hd:MOBA_V_ROWS, :] = ones_rows

    qt = q_ref[...].astype(F32).T
    row = lax.broadcasted_iota(I32, qt.shape, 0)
    km = km_ref[...]
    km_hi = km.astype(BF16)
    km_lo = (km - km_hi.astype(F32)).astype(BF16)
    jrow = lax.broadcasted_iota(I32, (n_blocks, nb), 0)
    for h in range(2):
        qb = jnp.where((row >= hd) if h else (row < hd), qt, 0.0).astype(BF16)
        qs_ref[h] = qb
        sc = (jnp.dot(km_hi, qb, preferred_element_type=F32)
              + jnp.dot(km_lo, qb, preferred_element_type=F32))
        sc = jnp.where(jrow < i, sc, NEG)
        cnt = jnp.zeros(sc.shape, I32)
        for jp in range(n_blocks):
            r = sc[jp:jp + 1, :]
            beats = (r > sc) | ((r == sc) & (jp < jrow))
            cnt = cnt + beats.astype(I32)
        sel = (jrow < i) & (cnt < MOBA_TOPK)
        bias = jnp.where(sel, 0.0, NEG)
        for jp in range(n_blocks):
            bias_ref[jp + 1, h:h + 1, :] = bias[jp:jp + 1, :]
    bias_ref[0, 0:2, :] = jnp.zeros((2, nb), F32)
    m_ref[0:2, :] = jnp.full((2, nb), NEG, F32)
    acc_ref[...] = jnp.zeros(acc_ref.shape, F32)

    key_idx = lax.broadcasted_iota(I32, (nb, nb), 0)
    qry_idx = lax.broadcasted_iota(I32, (nb, nb), 1)

    def scores_into(s_ref, blk, causal=False):
        kj = k_ref[pl.ds(pl.multiple_of(blk * nb, nb), nb), :]
        for h in range(2):
            s = jnp.dot(kj, qs_ref[h], preferred_element_type=F32)
            s_ref[h] = jnp.where(key_idx <= qry_idx, s, NEG) if causal else s

    def attend(s_ref, t, v_blk):
        quarter = nb // 4
        for h in range(2):
            brow = bias_ref[t, h:h + 1, :]
            parts = [s_ref[h, r * quarter:(r + 1) * quarter, :] for r in range(4)]
            tall = jnp.maximum(jnp.maximum(parts[0], parts[1]), jnp.maximum(parts[2], parts[3]))
            m_old = m_ref[h:h + 1, :]
            m_new = jnp.maximum(m_old, jnp.max(tall, axis=0, keepdims=True) + brow)
            alpha = jnp.exp2(m_old - m_new)
            offset = jnp.where(brow < 0.0, -NEG, m_new)
            p = jnp.exp2(s_ref[h] - offset).astype(BF16)
            pv = jnp.dot(vt_ref[v_blk, h], p, preferred_element_type=F32)
            acc_ref[h] = alpha * acc_ref[h] + pv
            m_ref[h:h + 1, :] = m_new

    bufs = (s0_ref, s1_ref, s2_ref, s3_ref)

    for r in range(n_sel_pages):
        page_copy(0, slot, r).wait()
    _sample_scores_fold(sel_c, page_buf.at[slot], qcol_ref, part_ref)

    scores_into(bufs[0], i, causal=True)
    for r in range(1, MOBA_GROUP):
        scores_into(bufs[r], r - 1)

    attend(bufs[0], 0, i)
    for r in range(1, MOBA_GROUP):
        attend(bufs[r], r, r - 1)

    def group_body(g, carry):
        base = MOBA_GROUP * g - 1
        for r in range(MOBA_GROUP):
            scores_into(bufs[r], base + r)
        for r in range(MOBA_GROUP):
            attend(bufs[r], base + r + 1, base + r)
        return carry

    lax.fori_loop(1, i // MOBA_GROUP + 1, group_body, 0)

    outs = [acc_ref[h, 0:hd, :] / acc_ref[h, hd:hd + 1, :] for h in range(2)]
    o_ref[...] = jnp.concatenate(outs, axis=0).T.astype(o_ref.dtype)

    @pl.when(sel_c == sel_steps - 1)
    def _():
        _sample_scores_emit(part_ref, score_ref, sel_blocks)


def _moba_prompt_and_sample_select(q, k, v, km, q_s, cache_kt, pt_flat, *, batch, seq, dec_batch, n_pages):
    nb = MOBA_BLOCK
    n_blocks = seq // nb
    pairs = A_HEADS // 2
    pw = 2 * A_HEAD_DIM
    n_steps = batch * pairs * n_blocks
    pages_per_step = dec_batch * n_pages // n_steps
    assert pages_per_step * n_steps == dec_batch * n_pages, "sample key pages must split evenly over the grid"
    assert pages_per_step % PAGES_PER_BLOCK == 0 and n_pages % pages_per_step == 0
    sel_steps = n_pages // pages_per_step

    def step_of(b, p, i):
        return (b * pairs + p) * n_blocks + i

    return pl.pallas_call(
        functools.partial(_moba_prompt_kernel, sel_steps=sel_steps, sel_blocks=n_pages // PAGES_PER_BLOCK),
        grid_spec=pltpu.PrefetchScalarGridSpec(
            num_scalar_prefetch=1,
            grid=(batch, pairs, n_blocks),
            in_specs=[
                pl.BlockSpec((nb, pw), lambda b, p, i, pt: (b * n_blocks + i, p)),
                pl.BlockSpec((seq, pw), lambda b, p, i, pt: (b, p)),
                pl.BlockSpec((1, pw, seq), lambda b, p, i, pt: (b, p, 0)),
                pl.BlockSpec((n_blocks, pw), lambda b, p, i, pt: (b, p)),
                pl.BlockSpec((1, 1, A_WIDTH), lambda b, p, i, pt: (step_of(b, p, i) // sel_steps, 0, 0)),
                pl.BlockSpec(memory_space=pl.ANY),
            ],
            out_specs=[
                pl.BlockSpec((nb, pw), lambda b, p, i, pt: (b * n_blocks + i, p)),
                pl.BlockSpec((1, A_HEADS, LANES), lambda b, p, i, pt: (step_of(b, p, i) // sel_steps, 0, 0)),
            ],
            scratch_shapes=[
                pltpu.VMEM((n_blocks, 2, MOBA_V_ROWS, nb), BF16),
                pltpu.VMEM((2, pw, nb), BF16),
                pltpu.VMEM((n_blocks + 1, SUBLANES, nb), F32),
                pltpu.VMEM((2, MOBA_V_ROWS, nb), F32),
                pltpu.VMEM((SUBLANES, nb), F32),
                pltpu.VMEM((A_WIDTH, PAGE_SIZE), F32),
                pltpu.VMEM((A_HEADS * SUBLANES, LANES), F32),
                pltpu.VMEM((2, pages_per_step, A_HEADS, A_HEAD_DIM, PAGE_SIZE), F32),
                pltpu.SemaphoreType.DMA((2,)),
            ] + [pltpu.VMEM((2, nb, nb), F32)] * MOBA_GROUP,
        ),
        out_shape=[
            jax.ShapeDtypeStruct((batch * seq, A_WIDTH), BF16),
            jax.ShapeDtypeStruct((dec_batch, A_HEADS, LANES), F32),
        ],
        compiler_params=_params("arbitrary", "arbitrary", "arbitrary"),
        name="moba_prompt",
    )(pt_flat, q, k, v, km, q_s.reshape(dec_batch, 1, A_WIDTH), cache_kt)


def _retention_prompt_kernel(q_ref, k_ref, v_ref, dmat_ref, qdec_ref, kdec_ref, y_ref, rfin_ref, r_ref,
                             *, state_decay):
    c = pl.program_id(1)
    kd, vd = R_KEY_DIM, R_VAL_DIM

    @pl.when(c == 0)
    def _():
        r_ref[...] = jnp.zeros(r_ref.shape, F32)

    chunk = q_ref.shape[0]
    lane = lax.broadcasted_iota(I32, (chunk, 2 * kd), 1)
    for p in range(R_HEADS // 2):
        q2 = q_ref[:, p * 2 * kd:(p + 1) * 2 * kd].astype(F32)
        k2 = k_ref[:, p * 2 * kd:(p + 1) * 2 * kd]
        qd2 = q2 * qdec_ref[p]
        kdt = (k2.astype(F32) * kdec_ref[p]).T.astype(BF16)
        r2 = r_ref[p]
        r2b = r2.astype(BF16)
        for hl in range(2):
            h = 2 * p + hl
            own = (lane >= kd) == bool(hl)
            qm = jnp.where(own, q2, 0.0).astype(BF16)
            qdm = jnp.where(own, qd2, 0.0).astype(BF16)
            sc = lax.dot_general(qm, k2, (((1,), (1,)), ((), ())), preferred_element_type=F32) * dmat_ref[h]
            vh = v_ref[:, h * vd:(h + 1) * vd]
            o = (jnp.dot(sc.astype(BF16), vh, preferred_element_type=F32)
                 + jnp.dot(qdm, r2b, preferred_element_type=F32))
            sl = slice(hl * kd, (hl + 1) * kd)
            u = jnp.dot(kdt[sl, :], vh, preferred_element_type=F32)
            r_ref[p, sl, :] = state_decay[h] * r2[sl, :] + u
            y = o * lax.rsqrt(jnp.mean(o * o, axis=-1, keepdims=True) + NORM_EPS)
            y_ref[:, h * vd:(h + 1) * vd] = y.astype(y_ref.dtype)

    @pl.when(c == pl.num_programs(1) - 1)
    def _():
        for h in range(R_HEADS):
            rfin_ref[0, h] = r_ref[h // 2, (h % 2) * kd:(h % 2 + 1) * kd, :]


def _retention_prompt(qr, kr, vr, *, batch, seq):
    chunk = RET_CHUNK
    nc = seq // chunk
    lg = _retention_log_decay()
    idx = jnp.arange(chunk, dtype=F32)
    diff = idx[:, None] - idx[None, :]
    dmat = jnp.where(diff >= 0, jnp.exp(lg[:, None, None] * jnp.maximum(diff, 0.0)), 0.0)
    qdec = jnp.exp(lg[:, None] * (idx + 1.0))
    kdec = jnp.exp(lg[:, None] * (chunk - 1.0 - idx))
    pair_table = lambda t: jnp.repeat(t.reshape(R_HEADS // 2, 2, chunk), R_KEY_DIM, axis=1).transpose(0, 2, 1)
    lg32 = np.log(np.float32(1.0) - np.exp2(np.float32(-5.0) - np.arange(R_HEADS, dtype=np.float32)))
    state_decay = tuple(float(np.exp(lg32[h] * np.float32(chunk))) for h in range(R_HEADS))
    const = lambda shape: pl.BlockSpec(shape, lambda b, c: (0,) * len(shape))
    return pl.pallas_call(
        functools.partial(_retention_prompt_kernel, state_decay=state_decay),
        grid=(batch, nc),
        in_specs=[
            pl.BlockSpec((chunk, R_QK_WIDTH), lambda b, c: (b * nc + c, 0)),
            pl.BlockSpec((chunk, R_QK_WIDTH), lambda b, c: (b * nc + c, 0)),
            pl.BlockSpec((chunk, R_V_WIDTH), lambda b, c: (b * nc + c, 0)),
            const((R_HEADS, chunk, chunk)),
            const((R_HEADS // 2, chunk, 2 * R_KEY_DIM)),
            const((R_HEADS // 2, chunk, 2 * R_KEY_DIM)),
        ],
        out_specs=[
            pl.BlockSpec((chunk, R_V_WIDTH), lambda b, c: (b * nc + c, 0)),
            pl.BlockSpec((1, R_HEADS, R_KEY_DIM, R_VAL_DIM), lambda b, c: (b, 0, 0, 0)),
        ],
        out_shape=[
            jax.ShapeDtypeStruct((batch * seq, R_V_WIDTH), BF16),
            jax.ShapeDtypeStruct((batch, R_HEADS, R_KEY_DIM, R_VAL_DIM), F32),
        ],
        scratch_shapes=[pltpu.VMEM((R_HEADS // 2, 2 * R_KEY_DIM, R_VAL_DIM), F32)],
        compiler_params=_params("arbitrary", "arbitrary"),
        name="retention_prompt",
    )(qr, kr, vr, dmat, pair_table(qdec), pair_table(kdec))


def _merge_kernel(x_ref, ya_ref, yr_ref, g_ref, wg_ref, wm_ref, b_ref, wpa_ref, wpr_ref, wo_ref, gp_ref, y_ref,
                  *, sub_tiles):
    rows = x_ref.shape[0] // sub_tiles
    for t in range(sub_tiles):
        sl = slice(t * rows, (t + 1) * rows)
        x = x_ref[sl, :]
        h = _rms_normed(x, g_ref[...]).astype(BF16)
        gates = jnp.dot(h, wg_ref[...], preferred_element_type=F32)
        ua = (ya_ref[sl, :].astype(F32) * jax.nn.silu(gates[:, :A_WIDTH])).astype(BF16)
        ur = (yr_ref[sl, :].astype(F32) * jax.nn.silu(gates[:, A_WIDTH:])).astype(BF16)
        pa = jnp.dot(ua, wpa_ref[...], preferred_element_type=F32)
        pr = jnp.dot(ur, wpr_ref[...], preferred_element_type=F32)
        mix = jnp.dot(h, wm_ref[...], preferred_element_type=F32) + b_ref[...]
        m = jax.nn.sigmoid(mix[:, :D_MODEL]) * pa + jax.nn.sigmoid(mix[:, D_MODEL:]) * pr
        o = jnp.dot(m.astype(BF16), wo_ref[...], preferred_element_type=F32)
        y_ref[sl, :] = x + _rms_normed(o, gp_ref[...])


def _merge(x2d, ya, yr, g_pre, w_g, w_m, b_merge, w_pa, w_pr, w_out, g_post, *, tm):
    t = x2d.shape[0]
    row = lambda width: pl.BlockSpec((tm, width), lambda i: (i, 0))
    const = lambda a: pl.BlockSpec(a.shape, lambda i: (0,) * a.ndim)
    return pl.pallas_call(
        functools.partial(_merge_kernel, sub_tiles=max(1, tm // MERGE_SUB_ROWS)),
        grid=(t // tm,),
        in_specs=[row(D_MODEL), row(A_WIDTH), row(R_V_WIDTH), const(g_pre), const(w_g), const(w_m),
                  const(b_merge), const(w_pa), const(w_pr), const(w_out), const(g_post)],
        out_specs=row(D_MODEL),
        out_shape=jax.ShapeDtypeStruct((t, D_MODEL), F32),
        compiler_params=_params("arbitrary"),
        name="gate_merge",
    )(x2d, ya, yr, g_pre, w_g, w_m, b_merge, w_pa, w_pr, w_out, g_post)


def _store_lane_repeated_column(row, col_ref):
    eye = lax.broadcasted_iota(I32, (LANES, LANES), 0) == lax.broadcasted_iota(I32, (LANES, LANES), 1)
    for j in range(row.shape[1] // LANES):
        wide = jnp.broadcast_to(row[:, j * LANES:(j + 1) * LANES], (LANES, LANES))
        col = jnp.sum(jnp.where(eye, wide, 0.0), axis=1, keepdims=True)
        col_ref[j * LANES:(j + 1) * LANES, :] = jnp.broadcast_to(col, (LANES, col_ref.shape[1]))


def _sample_scores_init(q_ref, qcol_ref, part_ref):
    part_ref[...] = jnp.zeros(part_ref.shape, F32)
    _store_lane_repeated_column(q_ref[0], qcol_ref)


def _sample_scores_fold(c, pages_ref, qcol_ref, part_ref):
    blocks_per_step = pages_ref.shape[0] // PAGES_PER_BLOCK
    groups = A_HEAD_DIM // SUBLANES
    folded = [[None] * A_HEADS for _ in range(blocks_per_step)]
    for h in range(A_HEADS):
        qh = qcol_ref[h * A_HEAD_DIM:(h + 1) * A_HEAD_DIM, :]
        for r in range(blocks_per_step):
            tot = pages_ref[r * PAGES_PER_BLOCK, h]
            for half in range(1, PAGES_PER_BLOCK):
                tot = tot + pages_ref[r * PAGES_PER_BLOCK + half, h]
            folded[r][h] = jnp.sum((tot * qh).reshape(groups, SUBLANES, PAGE_SIZE), axis=0)
    lane = lax.broadcasted_iota(I32, part_ref.shape, 1)
    part = part_ref[...]
    for r in range(blocks_per_step):
        col = jnp.sum(jnp.concatenate(folded[r], axis=0), axis=1, keepdims=True)
        part = jnp.where(lane == c * blocks_per_step + r, col, part)
    part_ref[...] = part


def _sample_scores_emit(part_ref, score_ref, n_blocks):
    sc = jnp.sum(part_ref[...].reshape(A_HEADS, SUBLANES, LANES), axis=1) * (1.0 / MOBA_BLOCK)
    blk = lax.broadcasted_iota(I32, sc.shape, 1)
    score_ref[0] = jnp.where(blk < n_blocks, sc, -jnp.inf)


def _topk_kernel(score_ref, idx_ref):
    sc = score_ref[...]
    blk = lax.broadcasted_iota(I32, sc.shape, 1)
    out = jnp.zeros(sc.shape, I32)
    for t in range(MOBA_TOPK):
        mx = jnp.max(sc, axis=1, keepdims=True)
        first = jnp.min(jnp.where(sc == mx, blk, LANES), axis=1, keepdims=True)
        out = jnp.where(blk == t, first, out)
        sc = jnp.where(blk == first, -jnp.inf, sc)
    idx_ref[...] = out


def _select_topk(scores):
    rows = scores.shape[0] * scores.shape[1]
    return pl.pallas_call(
        _topk_kernel,
        out_shape=jax.ShapeDtypeStruct((rows, LANES), I32),
        compiler_params=pltpu.CompilerParams(vmem_limit_bytes=VMEM_LIMIT_BYTES),
        name="select_topk",
    )(scores.reshape(rows, LANES))


def _sample_attn_kernel(sel_ref, pt_ref, q_ref, ks_ref, vs_ref, ck_ref, cv_ref, o_ref, kbuf, vbuf, sems,
                        qcol_ref, ocol_ref, *, n_pages):
    b = pl.program_id(0)
    hd = A_HEAD_DIM
    slot = b % 2

    def slab_copies(page, head, dst_slot, j):
        return (pltpu.make_async_copy(ck_ref.at[page, head], kbuf.at[dst_slot, head, j], sems.at[dst_slot, 0]),
                pltpu.make_async_copy(cv_ref.at[page, head], vbuf.at[dst_slot, head, j], sems.at[dst_slot, 1]))

    def start_fetch(seq, dst_slot):
        for head in range(A_HEADS):
            for t in range(MOBA_TOPK):
                blk = sel_ref[(seq * A_HEADS + head) * MOBA_TOPK + t]
                for half in range(PAGES_PER_BLOCK):
                    page = pt_ref[seq * n_pages + blk * PAGES_PER_BLOCK + half]
                    for copy in slab_copies(page, head, dst_slot, t * PAGES_PER_BLOCK + half):
                        copy.start()

    @pl.when(b == 0)
    def _():
        start_fetch(0, 0)

    @pl.when(b + 1 < pl.num_programs(0))
    def _():
        start_fetch(b + 1, 1 - slot)

    for head in range(A_HEADS):
        for j in range(SAMPLE_SLABS):
            for copy in slab_copies(0, head, slot, j):
                copy.wait()

    groups = hd // SUBLANES
    q_row = q_ref[0] * (hd ** -0.5)
    _store_lane_repeated_column(q_row, qcol_ref)
    head_row = lax.broadcasted_iota(I32, (A_HEADS, PAGE_SIZE), 0)
    lane_head = lax.broadcasted_iota(I32, (1, A_WIDTH), 1) // hd

    def heads_to_lanes(col):
        row = jnp.zeros((1, A_WIDTH), F32)
        for head in range(A_HEADS):
            row = jnp.where(lane_head == head, col[head:head + 1, :], row)
        return row

    scores = []
    for j in range(SAMPLE_SLABS):
        folded = []
        for head in range(A_HEADS):
            prod = kbuf[slot, head, j] * qcol_ref[head * hd:(head + 1) * hd, :]
            folded.append(jnp.sum(prod.reshape(groups, SUBLANES, PAGE_SIZE), axis=0))
        stacked = jnp.concatenate(folded, axis=0).reshape(A_HEADS, SUBLANES, PAGE_SIZE)
        scores.append(jnp.sum(stacked, axis=1))
    qk_self = q_row * ks_ref[0]
    s_self = jnp.zeros((A_HEADS, 1), F32)
    for head in range(A_HEADS):
        part = jnp.sum(jnp.where(lane_head == head, qk_self, 0.0), axis=1, keepdims=True)
        s_self = jnp.where(head_row[:, :1] == head, part, s_self)

    m = scores[0]
    for sp in scores[1:]:
        m = jnp.maximum(m, sp)
    m = jnp.maximum(jnp.max(m, axis=1, keepdims=True), s_self)
    probs = [jnp.exp(sp - m) for sp in scores]
    p_self = jnp.exp(s_self - m)
    p_sum = probs[0]
    for p in probs[1:]:
        p_sum = p_sum + p
    l = jnp.sum(p_sum, axis=1, keepdims=True) + p_self

    for head in range(A_HEADS):
        acc = None
        for j in range(SAMPLE_SLABS):
            term = vbuf[slot, head, j] * probs[j][head:head + 1, :]
            acc = term if acc is None else acc + term
        ocol_ref[head * hd:(head + 1) * hd, :] = jnp.sum(acc, axis=1, keepdims=True)

    eye = lax.broadcasted_iota(I32, (LANES, LANES), 0) == lax.broadcasted_iota(I32, (LANES, LANES), 1)
    out_chunks = []
    for c in range(A_WIDTH // LANES):
        col = jnp.broadcast_to(ocol_ref[c * LANES:(c + 1) * LANES, :], (LANES, LANES))
        out_chunks.append(jnp.sum(jnp.where(eye, col, 0.0), axis=0, keepdims=True))
    out_row = jnp.concatenate(out_chunks, axis=1)
    o_ref[0] = (out_row + heads_to_lanes(p_self) * vs_ref[0]) / heads_to_lanes(l)


def _sample_attention(sel_flat, pt_flat, q_s, k_s, v_s, cache_kt, cache_vt, *, batch, n_pages):
    tok_spec = pl.BlockSpec((1, 1, A_WIDTH), lambda b, sel, pt: (b, 0, 0))
    slab_buf = pltpu.VMEM((2, A_HEADS, SAMPLE_SLABS, A_HEAD_DIM, PAGE_SIZE), F32)
    return pl.pallas_call(
        functools.partial(_sample_attn_kernel, n_pages=n_pages),
        grid_spec=pltpu.PrefetchScalarGridSpec(
            num_scalar_prefetch=2,
            grid=(batch,),
            in_specs=[tok_spec] * 3 + [pl.BlockSpec(memory_space=pl.ANY)] * 2,
            out_specs=tok_spec,
            scratch_shapes=[slab_buf, slab_buf, pltpu.SemaphoreType.DMA((2, 2)),
                            pltpu.VMEM((A_WIDTH, PAGE_SIZE), F32),
                            pltpu.VMEM((A_WIDTH, 1), F32)],
        ),
        out_shape=jax.ShapeDtypeStruct((batch, 1, A_WIDTH), F32),
        compiler_params=_params("arbitrary"),
        name="sample_attention",
    )(sel_flat, pt_flat, q_s.reshape(batch, 1, A_WIDTH), k_s.reshape(batch, 1, A_WIDTH),
      v_s.reshape(batch, 1, A_WIDTH), cache_kt, cache_vt)


def _sample_retention_kernel(q_ref, k_ref, v_ref, r_ref, y_ref, rnew_ref, *, decay):
    bb = q_ref.shape[0]
    kd, vd = R_KEY_DIM, R_VAL_DIM
    reps = LANES // bb
    qt = jnp.concatenate([q_ref[...]] * reps, axis=0).T
    kt = jnp.concatenate([k_ref[...]] * reps, axis=0).T
    for bi in range(bb):
        for h in range(R_HEADS):
            rows = slice(h * kd, (h + 1) * kd)
            qc = qt[rows, bi:bi + 1]
            kc = kt[rows, bi:bi + 1]
            vrow = v_ref[bi:bi + 1, h * vd:(h + 1) * vd]
            r = r_ref[bi, rows, :]
            qk = jnp.sum(qc * kc, axis=0, keepdims=True)
            cross = jnp.sum((qc * decay[h]) * r, axis=0, keepdims=True)
            o = qk * vrow + cross
            rnew_ref[bi, rows, :] = decay[h] * r + kc * vrow
            y = o * lax.rsqrt(jnp.mean(o * o, axis=-1, keepdims=True) + NORM_EPS)
            y_ref[bi:bi + 1, h * vd:(h + 1) * vd] = y


def _sample_retention(qr_s, kr_s, vr_s, state):
    batch = qr_s.shape[0]
    bb = SAMPLE_RET_BATCH
    lg32 = np.log(np.float32(1.0) - np.exp2(np.float32(-5.0) - np.arange(R_HEADS, dtype=np.float32)))
    decay = tuple(float(np.exp(lg32[h])) for h in range(R_HEADS))
    state3 = state.reshape(batch, R_HEADS * R_KEY_DIM, R_VAL_DIM)
    y, r_new = pl.pallas_call(
        functools.partial(_sample_retention_kernel, decay=decay),
        grid=(batch // bb,),
        in_specs=[
            pl.BlockSpec((bb, R_QK_WIDTH), lambda i: (i, 0)),
            pl.BlockSpec((bb, R_QK_WIDTH), lambda i: (i, 0)),
            pl.BlockSpec((bb, R_V_WIDTH), lambda i: (i, 0)),
            pl.BlockSpec((bb, R_HEADS * R_KEY_DIM, R_VAL_DIM), lambda i: (i, 0, 0)),
        ],
        out_specs=[
            pl.BlockSpec((bb, R_V_WIDTH), lambda i: (i, 0)),
            pl.BlockSpec((bb, R_HEADS * R_KEY_DIM, R_VAL_DIM), lambda i: (i, 0, 0)),
        ],
        out_shape=[
            jax.ShapeDtypeStruct((batch, R_V_WIDTH), F32),
            jax.ShapeDtypeStruct(state3.shape, F32),
        ],
        compiler_params=_params("arbitrary"),
        name="sample_retention",
    )(qr_s, kr_s, vr_s, state3)
    return y, r_new.reshape(state.shape)


def _layer(x_p, x_s, cache_k, cache_v, state_r, page_table, g_pre, w_in, b_merge, w_pa, w_pr, w_out, g_post):
    batch, seq, _ = x_p.shape
    dec_batch, dec_seq, _ = x_s.shape
    assert dec_seq == 1
    n_pages = page_table.shape[1]
    past_len = n_pages * PAGE_SIZE
    assert seq % PROMPT_TOKENS_PER_TILE == 0 and seq % RET_CHUNK == 0 and seq % (MOBA_GROUP * MOBA_BLOCK) == 0
    assert MOBA_TOPK <= n_pages // PAGES_PER_BLOCK <= LANES
    assert dec_batch % SAMPLE_RET_BATCH == 0

    o = _OFFS
    w_bf = w_in.astype(BF16)
    w_a = jnp.concatenate([w_bf[:, o[0]:o[3]], w_bf[:, o[4]:o[7]]], axis=1)
    w_g = jnp.concatenate([w_bf[:, o[3]:o[4]], w_bf[:, o[7]:o[8]]], axis=1)
    w_m = w_bf[:, o[8]:o[10]]
    w_pa_b, w_pr_b, w_out_b = w_pa.astype(BF16), w_pr.astype(BF16), w_out.astype(BF16)
    g_pre2, g_post2, b2 = g_pre.reshape(1, D_MODEL), g_post.reshape(1, D_MODEL), b_merge.reshape(1, 2 * D_MODEL)

    pos_p = jnp.arange(seq, dtype=I32)
    xp2 = x_p.reshape(batch * seq, D_MODEL)
    q, kf_t, kb, vf_t, vb_t, qr, kr, vr, km = _in_projection(
        xp2, g_pre2, w_a,
        _rope_tables(pos_p, ROPE_DIM, A_HEAD_DIM, ROPE_THETA),
        _rope_tables(pos_p, R_KEY_DIM, R_KEY_DIM, R_ROPE_THETA),
        tm=PROMPT_TOKENS_PER_TILE, prompt_seq=seq, act_dtype=BF16, q_scale=MOBA_Q_SCALE)
    km = km.reshape(batch * (seq // MOBA_BLOCK), A_WIDTH)
    pos_s = jnp.full((dec_batch,), past_len, dtype=I32)
    xs2 = x_s.reshape(dec_batch, D_MODEL)
    q_s, k_s, _, v_s, _, qr_s, kr_s, vr_s = _in_projection(
        xs2, g_pre2, w_a,
        _rope_tables(pos_s, ROPE_DIM, A_HEAD_DIM, ROPE_THETA),
        _rope_tables(pos_s, R_KEY_DIM, R_KEY_DIM, R_ROPE_THETA),
        tm=dec_batch, prompt_seq=None, act_dtype=F32, q_scale=None)
    cache_kt = cache_k.transpose(0, 2, 3, 1)
    cache_vt = cache_v.transpose(0, 2, 3, 1)
    pt_flat = page_table.reshape(-1)

    ya, block_scores = _moba_prompt_and_sample_select(q, kb, vb_t, km, q_s, cache_kt, pt_flat, batch=batch, seq=seq,
                                                      dec_batch=dec_batch, n_pages=n_pages)
    yr, r_p = _retention_prompt(qr, kr, vr, batch=batch, seq=seq)
    y_p = _merge(xp2, ya, yr, g_pre2, w_g, w_m, b2, w_pa_b, w_pr_b, w_out_b, g_post2, tm=PROMPT_TOKENS_PER_TILE)

    sel_flat = _select_topk(block_scores)[:, :MOBA_TOPK].reshape(-1)
    ya_s = _sample_attention(sel_flat, pt_flat, q_s, k_s, v_s, cache_kt, cache_vt,
                             batch=dec_batch, n_pages=n_pages).reshape(dec_batch, A_WIDTH)
    yr_s, r_s = _sample_retention(qr_s, kr_s, vr_s, state_r)
    y_s = _merge(xs2, ya_s, yr_s, g_pre2, w_g, w_m, b2, w_pa_b, w_pr_b, w_out_b, g_post2, tm=dec_batch)

    per_token = lambda a: a.reshape(batch, A_HEADS, A_HEAD_DIM, seq).transpose(0, 3, 1, 2)
    return (y_p.reshape(x_p.shape), y_s.reshape(x_s.shape), per_token(kf_t), per_token(vf_t), r_p,
            k_s.reshape(dec_batch, 1, A_HEADS, A_HEAD_DIM), v_s.reshape(dec_batch, 1, A_HEADS, A_HEAD_DIM), r_s)


def kernel(x_prompt, x_sample, cache_k, cache_v, state_ret, page_table, norm_pre_g, w_in, b_merge, w_proj_a,
           w_proj_r, w_out, norm_post_g):
    h_p, h_s = x_prompt, x_sample
    outs = [[] for _ in range(6)]
    for layer in range(w_in.shape[0]):
        res = _layer(h_p, h_s, cache_k[layer], cache_v[layer], state_ret[layer], page_table,
                     norm_pre_g[layer], w_in[layer], b_merge[layer], w_proj_a[layer], w_proj_r[layer],
                     w_out[layer], norm_post_g[layer])
        h_p, h_s = res[0], res[1]
        for acc, leaf in zip(outs, res[2:]):
            acc.append(leaf)
    return (h_p, h_s) + tuple(jnp.stack(o) for o in outs)
```

```python
import functools

import numpy as np
import jax
import jax.numpy as jnp
from jax import lax
from jax.experimental import pallas as pl
from jax.experimental.pallas import tpu as pltpu

F32 = jnp.float32
BF16 = jnp.bfloat16
I32 = jnp.int32

D_MODEL = 1024
A_HEADS = 8
A_HEAD_DIM = 64
A_WIDTH = A_HEADS * A_HEAD_DIM
MOBA_BLOCK = 256
MOBA_TOPK = 3
ROPE_THETA = 500000.0
ROPE_DIM = A_HEAD_DIM // 4
R_HEADS = 4
R_KEY_DIM = 64
R_VAL_DIM = 128
R_QK_WIDTH = R_HEADS * R_KEY_DIM
R_V_WIDTH = R_HEADS * R_VAL_DIM
R_ROPE_THETA = 10000.0
PAGE_SIZE = 128
NORM_EPS = 1e-6
NEG = -1e30

_OFFS = np.cumsum((0, A_WIDTH, A_WIDTH, A_WIDTH, A_WIDTH, R_QK_WIDTH, R_QK_WIDTH, R_V_WIDTH, R_V_WIDTH, D_MODEL, D_MODEL))

LANES = 128
SUBLANES = 8
VMEM_LIMIT_BYTES = 56 * 1024 * 1024

MOBA_Q_SCALE = A_HEAD_DIM ** -0.5 * float(np.log2(np.e))
MOBA_GROUP = 4
BF16_SUBLANES = 16
MOBA_V_ROWS = A_HEAD_DIM + BF16_SUBLANES
RET_CHUNK = 256
RET_SEQS_PER_STEP = 8
PROMPT_TOKENS_PER_TILE = 512
MERGE_SUB_ROWS = 256
PAGES_PER_BLOCK = MOBA_BLOCK // PAGE_SIZE
SAMPLE_SLABS = MOBA_TOPK * PAGES_PER_BLOCK
SAMPLE_RET_BATCH = 16


def _params(*semantics):
    return pltpu.CompilerParams(dimension_semantics=semantics, vmem_limit_bytes=VMEM_LIMIT_BYTES)


def _rms_normed(x, g):
    return x * lax.rsqrt(jnp.mean(x * x, axis=-1, keepdims=True) + NORM_EPS) * g


def _rope_tables(pos, rot_dim, head_dim, theta):
    half = rot_dim // 2
    inv_freq = theta ** (-jnp.arange(half, dtype=F32) * (2.0 / rot_dim))
    d = jnp.arange(LANES) % head_dim
    ang = pos.astype(F32)[:, None] * inv_freq[d % half][None, :]
    cos, sin = jnp.cos(ang), jnp.sin(ang)
    c = jnp.where(d < rot_dim, cos, 1.0)
    s1 = jnp.where(d < half, -sin, 0.0)
    s2 = jnp.where((d >= half) & (d < rot_dim), sin, 0.0)
    return c, s1, s2


def _retention_log_decay():
    return jnp.log(1.0 - jnp.exp2(-5.0 - jnp.arange(R_HEADS, dtype=F32)))


def _rope_chunks(z, rs, c_ref, s1_ref, s2_ref, half):
    c, s1, s2 = c_ref[rs, :], s1_ref[rs, :], s2_ref[rs, :]
    out = []
    for j in range(z.shape[1] // LANES):
        zs = z[:, j * LANES:(j + 1) * LANES]
        up = pltpu.roll(zs, LANES - half, 1)
        down = pltpu.roll(zs, half, 1)
        out.append(zs * c + up * s1 + down * s2)
    return out


def _store_chunks(ref, rs, chunks, scale=None):
    for j, ch in enumerate(chunks):
        if scale is not None:
            ch = ch * scale
        ref[rs, j * LANES:(j + 1) * LANES] = ch.astype(ref.dtype)


def _store_chunks_token_minor(refs, rs, chunks):
    for j, ch in enumerate(chunks):
        ct = ch.T
        for ref in refs:
            ref[0, j * LANES:(j + 1) * LANES, rs] = ct.astype(ref.dtype)


def _inproj_kernel(x_ref, g_ref, w_ref, ca_ref, sa1_ref, sa2_ref, cr_ref, sr1_ref, sr2_ref,
                   q_ref, kf_ref, kb_ref, vf_ref, vb_ref, qr_ref, kr_ref, vr_ref, *km_refs, q_scale, token_minor_kv,
                   sub_tiles):
    a, rq, rv = A_WIDTH, R_QK_WIDTH, R_V_WIDTH
    rows = x_ref.shape[0] // sub_tiles
    for t in range(sub_tiles):
        rs = slice(t * rows, (t + 1) * rows)
        h = _rms_normed(x_ref[rs, :], g_ref[...]).astype(BF16)

        def proj(lo, hi, h=h):
            return jnp.dot(h, w_ref[:, lo:hi], preferred_element_type=F32)

        _store_chunks(q_ref, rs, _rope_chunks(proj(0, a), rs, ca_ref, sa1_ref, sa2_ref, ROPE_DIM // 2),
                      scale=q_scale)
        k_chunks = _rope_chunks(proj(a, 2 * a), rs, ca_ref, sa1_ref, sa2_ref, ROPE_DIM // 2)
        if token_minor_kv:
            _store_chunks_token_minor([kf_ref], rs, k_chunks)
        else:
            _store_chunks(kf_ref, rs, k_chunks)
        _store_chunks(kb_ref, rs, k_chunks)
        if km_refs:
            (km_ref,) = km_refs
            blocks = rows // MOBA_BLOCK
            for j, ch in enumerate(k_chunks):
                for blk in range(blocks):
                    part = ch[blk * MOBA_BLOCK:(blk + 1) * MOBA_BLOCK, :]
                    km_ref[0, t * blocks + blk:t * blocks + blk + 1, j * LANES:(j + 1) * LANES] = (
                        jnp.sum(part, axis=0, keepdims=True) * (1.0 / MOBA_BLOCK))
        v = proj(2 * a, 3 * a)
        if token_minor_kv:
            _store_chunks_token_minor([vf_ref, vb_ref], rs,
                                      [v[:, j * LANES:(j + 1) * LANES] for j in range(a // LANES)])
        else:
            vf_ref[rs, :] = v
            vb_ref[rs, :] = v.astype(vb_ref.dtype)
        o = 3 * a
        _store_chunks(qr_ref, rs, _rope_chunks(proj(o, o + rq), rs, cr_ref, sr1_ref, sr2_ref, R_KEY_DIM // 2))
        _store_chunks(kr_ref, rs,
                      _rope_chunks(proj(o + rq, o + 2 * rq), rs, cr_ref, sr1_ref, sr2_ref, R_KEY_DIM // 2),
                      scale=R_KEY_DIM ** -0.5)
        vr_ref[rs, :] = proj(o + 2 * rq, o + 2 * rq + rv).astype(vr_ref.dtype)


def _in_projection(x2d, g_pre, w_a, tabs_a, tabs_r, *, tm, prompt_seq, act_dtype, q_scale):
    t = x2d.shape[0]
    n = t // tm
    tab_blocks = tabs_a[0].shape[0] // tm
    tab_spec = pl.BlockSpec((tm, LANES), lambda i: (i % tab_blocks, 0))
    row = lambda width: pl.BlockSpec((tm, width), lambda i: (i, 0))
    const = lambda shape: pl.BlockSpec(shape, lambda i: (0,) * len(shape))
    if prompt_seq is None:
        kv_shape, kv_spec = (t, A_WIDTH), row(A_WIDTH)
    else:
        seq_blocks = prompt_seq // tm
        kv_shape = (t // prompt_seq, A_WIDTH, prompt_seq)
        kv_spec = pl.BlockSpec((1, A_WIDTH, tm), lambda i: (i // seq_blocks, 0, i % seq_blocks))
    out_shape = [
        jax.ShapeDtypeStruct((t, A_WIDTH), act_dtype),
        jax.ShapeDtypeStruct(kv_shape, F32),
        jax.ShapeDtypeStruct((t, A_WIDTH), BF16),
        jax.ShapeDtypeStruct(kv_shape, F32),
        jax.ShapeDtypeStruct(kv_shape, BF16),
        jax.ShapeDtypeStruct((t, R_QK_WIDTH), act_dtype),
        jax.ShapeDtypeStruct((t, R_QK_WIDTH), act_dtype),
        jax.ShapeDtypeStruct((t, R_V_WIDTH), act_dtype),
    ]
    out_specs = [row(A_WIDTH), kv_spec, row(A_WIDTH), kv_spec, kv_spec] + [row(R_QK_WIDTH)] * 2 + [row(R_V_WIDTH)]
    if prompt_seq is not None:
        nb = tm // MOBA_BLOCK
        out_shape.append(jax.ShapeDtypeStruct((n, nb, A_WIDTH), F32))
        out_specs.append(pl.BlockSpec((1, nb, A_WIDTH), lambda i: (i, 0, 0)))
    return pl.pallas_call(
        functools.partial(_inproj_kernel, q_scale=q_scale, token_minor_kv=prompt_seq is not None,
                          sub_tiles=1),
        grid=(n,),
        in_specs=[row(D_MODEL), const((1, D_MODEL)), const(w_a.shape)] + [tab_spec] * 6,
        out_specs=out_specs,
        out_shape=out_shape,
        compiler_params=_params("arbitrary"),
        name="in_projection",
    )(x2d, g_pre, w_a, *tabs_a, *tabs_r)


def _moba_prompt_kernel(pt_ref, q_ref, k_ref, v_ref, km_ref, qsel_ref, ck_ref, o_ref, score_ref,
                        vt_ref, qs_ref, bias_ref, acc_ref, m_ref, qcol_ref, part_ref, page_buf, page_sem,
                        s0_ref, s1_ref, s2_ref, s3_ref, *, sel_steps, sel_blocks):
    i = pl.program_id(2)
    nb = MOBA_BLOCK
    hd = A_HEAD_DIM
    n_blocks = k_ref.shape[0] // nb
    n_sel_pages = page_buf.shape[1]

    step = (pl.program_id(0) * pl.num_programs(1) + pl.program_id(1)) * pl.num_programs(2) + i
    n_steps = pl.num_programs(0) * pl.num_programs(1) * pl.num_programs(2)
    slot = step % 2
    sel_c = step % sel_steps

    def page_copy(page, dst_slot, r):
        return pltpu.make_async_copy(ck_ref.at[page], page_buf.at[dst_slot, r], page_sem.at[dst_slot])

    def start_pages(s, dst_slot):
        for r in range(n_sel_pages):
            page_copy(pt_ref[s * n_sel_pages + r], dst_slot, r).start()

    @pl.when(step == 0)
    def _():
        start_pages(0, 0)

    @pl.when(step + 1 < n_steps)
    def _():
        start_pages(step + 1, 1 - slot)

    @pl.when(sel_c == 0)
    def _():
        _sample_scores_init(qsel_ref, qcol_ref, part_ref)

    @pl.when(i == 0)
    def _():
        ones_rows = (lax.broadcasted_iota(I32, (MOBA_V_ROWS - hd, nb), 0) == 0).astype(BF16)
        for c in range(n_blocks):
            for h in range(2):
                vt_ref[c, h, 0:hd, :] = v_ref[0, h * hd:(h + 1) * hd, c * nb:(c + 1) * nb]
                vt_ref[c, h, hd:MOBA_V_ROWS, :] = ones_rows

    qt = q_ref[...].astype(F32).T
    row = lax.broadcasted_iota(I32, qt.shape, 0)
    km = km_ref[...]
    km_hi = km.astype(BF16)
    km_lo = (km - km_hi.astype(F32)).astype(BF16)
    jrow = lax.broadcasted_iota(I32, (n_blocks, nb), 0)
    for h in range(2):
        qb = jnp.where((row >= hd) if h else (row < hd), qt, 0.0).astype(BF16)
        qs_ref[h] = qb
        sc = (jnp.dot(km_hi, qb, preferred_element_type=F32)
              + jnp.dot(km_lo, qb, preferred_element_type=F32))
        sc = jnp.where(jrow < i, sc, NEG)
        cnt = jnp.zeros(sc.shape, I32)
        for jp in range(n_blocks):
            r = sc[jp:jp + 1, :]
            beats = (r > sc) | ((r == sc) & (jp < jrow))
            cnt = cnt + beats.astype(I32)
        sel = (jrow < i) & (cnt < MOBA_TOPK)
        bias = jnp.where(sel, 0.0, NEG)
        for jp in range(n_blocks):
            bias_ref[jp + 1, h:h + 1, :] = bias[jp:jp + 1, :]
    bias_ref[0, 0:2, :] = jnp.zeros((2, nb), F32)
    m_ref[0:2, :] = jnp.full((2, nb), NEG, F32)
    acc_ref[...] = jnp.zeros(acc_ref.shape, F32)

    key_idx = lax.broadcasted_iota(I32, (nb, nb), 0)
    qry_idx = lax.broadcasted_iota(I32, (nb, nb), 1)

    def scores_into(s_ref, blk, causal=False):
        kj = k_ref[pl.ds(pl.multiple_of(blk * nb, nb), nb), :]
        for h in range(2):
            s = jnp.dot(kj, qs_ref[h], preferred_element_type=F32)
            s_ref[h] = jnp.where(key_idx <= qry_idx, s, NEG) if causal else s

    def attend(s_ref, t, v_blk):
        quarter = nb // 4
        for h in range(2):
            brow = bias_ref[t, h:h + 1, :]
            parts = [s_ref[h, r * quarter:(r + 1) * quarter, :] for r in range(4)]
            tall = jnp.maximum(jnp.maximum(parts[0], parts[1]), jnp.maximum(parts[2], parts[3]))
            m_old = m_ref[h:h + 1, :]
            m_new = jnp.maximum(m_old, jnp.max(tall, axis=0, keepdims=True) + brow)
            alpha = jnp.exp2(m_old - m_new)
            offset = jnp.where(brow < 0.0, -NEG, m_new)
            p = jnp.exp2(s_ref[h] - offset).astype(BF16)
            pv = jnp.dot(vt_ref[v_blk, h], p, preferred_element_type=F32)
            acc_ref[h] = alpha * acc_ref[h] + pv
            m_ref[h:h + 1, :] = m_new

    bufs = (s0_ref, s1_ref, s2_ref, s3_ref)

    for r in range(n_sel_pages):
        page_copy(0, slot, r).wait()
    _sample_scores_fold(sel_c, page_buf.at[slot], qcol_ref, part_ref)

    scores_into(bufs[0], i, causal=True)
    for r in range(1, MOBA_GROUP):
        scores_into(bufs[r], r - 1)

    attend(bufs[0], 0, i)
    for r in range(1, MOBA_GROUP):
        attend(bufs[r], r, r - 1)

    def group_body(g, carry):
        base = MOBA_GROUP * g - 1
        for r in range(MOBA_GROUP):
            scores_into(bufs[r], base + r)
        for r in range(MOBA_GROUP):
            attend(bufs[r], base + r + 1, base + r)
        return carry

    lax.fori_loop(1, i // MOBA_GROUP + 1, group_body, 0)

    outs = [acc_ref[h, 0:hd, :] / acc_ref[h, hd:hd + 1, :] for h in range(2)]
    o_ref[...] = jnp.concatenate(outs, axis=0).T.astype(o_ref.dtype)

    @pl.when(sel_c == sel_steps - 1)
    def _():
        _sample_scores_emit(part_ref, score_ref, sel_blocks)


def _moba_prompt_and_sample_select(q, k, v, km, q_s, cache_kt, pt_flat, *, batch, seq, dec_batch, n_pages):
    nb = MOBA_BLOCK
    n_blocks = seq // nb
    pairs = A_HEADS // 2
    pw = 2 * A_HEAD_DIM
    n_steps = batch * pairs * n_blocks
    pages_per_step = dec_batch * n_pages // n_steps
    assert pages_per_step * n_steps == dec_batch * n_pages, "sample key pages must split evenly over the grid"
    assert pages_per_step % PAGES_PER_BLOCK == 0 and n_pages % pages_per_step == 0
    sel_steps = n_pages // pages_per_step

    def step_of(b, p, i):
        return (b * pairs + p) * n_blocks + i

    return pl.pallas_call(
        functools.partial(_moba_prompt_kernel, sel_steps=sel_steps, sel_blocks=n_pages // PAGES_PER_BLOCK),
        grid_spec=pltpu.PrefetchScalarGridSpec(
            num_scalar_prefetch=1,
            grid=(batch, pairs, n_blocks),
            in_specs=[
                pl.BlockSpec((nb, pw), lambda b, p, i, pt: (b * n_blocks + i, p)),
                pl.BlockSpec((seq, pw), lambda b, p, i, pt: (b, p)),
                pl.BlockSpec((1, pw, seq), lambda b, p, i, pt: (b, p, 0)),
                pl.BlockSpec((n_blocks, pw), lambda b, p, i, pt: (b, p)),
                pl.BlockSpec((1, 1, A_WIDTH), lambda b, p, i, pt: (step_of(b, p, i) // sel_steps, 0, 0)),
                pl.BlockSpec(memory_space=pl.ANY),
            ],
            out_specs=[
                pl.BlockSpec((nb, pw), lambda b, p, i, pt: (b * n_blocks + i, p)),
                pl.BlockSpec((1, A_HEADS, LANES), lambda b, p, i, pt: (step_of(b, p, i) // sel_steps, 0, 0)),
            ],
            scratch_shapes=[
                pltpu.VMEM((n_blocks, 2, MOBA_V_ROWS, nb), BF16),
                pltpu.VMEM((2, pw, nb), BF16),
                pltpu.VMEM((n_blocks + 1, SUBLANES, nb), F32),
                pltpu.VMEM((2, MOBA_V_ROWS, nb), F32),
                pltpu.VMEM((SUBLANES, nb), F32),
                pltpu.VMEM((A_WIDTH, PAGE_SIZE), F32),
                pltpu.VMEM((A_HEADS * SUBLANES, LANES), F32),
                pltpu.VMEM((2, pages_per_step, A_HEADS, A_HEAD_DIM, PAGE_SIZE), F32),
                pltpu.SemaphoreType.DMA((2,)),
            ] + [pltpu.VMEM((2, nb, nb), F32)] * MOBA_GROUP,
        ),
        out_shape=[
            jax.ShapeDtypeStruct((batch * seq, A_WIDTH), BF16),
            jax.ShapeDtypeStruct((dec_batch, A_HEADS, LANES), F32),
        ],
        compiler_params=_params("arbitrary", "arbitrary", "arbitrary"),
        name="moba_prompt",
    )(pt_flat, q, k, v, km, q_s.reshape(dec_batch, 1, A_WIDTH), cache_kt)


def _retention_prompt_kernel(q_ref, k_ref, v_ref, dmat_ref, qdec_ref, kdec_ref, y_ref, rfin_ref, r_ref,
                             *, state_decay):
    c = pl.program_id(1)
    kd, vd = R_KEY_DIM, R_VAL_DIM

    @pl.when(c == 0)
    def _():
        r_ref[...] = jnp.zeros(r_ref.shape, F32)

    n_seqs, chunk = q_ref.shape[0], q_ref.shape[1]
    lane = lax.broadcasted_iota(I32, (chunk, 2 * kd), 1)
    for s in range(n_seqs):
        for p in range(R_HEADS // 2):
            q2 = q_ref[s, :, p * 2 * kd:(p + 1) * 2 * kd].astype(F32)
            k2 = k_ref[s, :, p * 2 * kd:(p + 1) * 2 * kd]
            qd2 = q2 * qdec_ref[p]
            kdt = (k2.astype(F32) * kdec_ref[p]).T.astype(BF16)
            r2 = r_ref[s, p]
            r2b = r2.astype(BF16)
            for hl in range(2):
                h = 2 * p + hl
                own = (lane >= kd) == bool(hl)
                qm = jnp.where(own, q2, 0.0).astype(BF16)
                qdm = jnp.where(own, qd2, 0.0).astype(BF16)
                sc = lax.dot_general(qm, k2, (((1,), (1,)), ((), ())), preferred_element_type=F32) * dmat_ref[h]
                vh = v_ref[s, :, h * vd:(h + 1) * vd]
                o = (jnp.dot(sc.astype(BF16), vh, preferred_element_type=F32)
                     + jnp.dot(qdm, r2b, preferred_element_type=F32))
                sl = slice(hl * kd, (hl + 1) * kd)
                u = jnp.dot(kdt[sl, :], vh, preferred_element_type=F32)
                r_ref[s, p, sl, :] = state_decay[h] * r2[sl, :] + u
                y = o * lax.rsqrt(jnp.mean(o * o, axis=-1, keepdims=True) + NORM_EPS)
                y_ref[s, :, h * vd:(h + 1) * vd] = y.astype(y_ref.dtype)

    @pl.when(c == pl.num_programs(1) - 1)
    def _():
        for s in range(n_seqs):
            for h in range(R_HEADS):
                rfin_ref[s, h] = r_ref[s, h // 2, (h % 2) * kd:(h % 2 + 1) * kd, :]


def _retention_prompt(qr, kr, vr, *, batch, seq):
    chunk = RET_CHUNK
    nc = seq // chunk
    lg = _retention_log_decay()
    idx = jnp.arange(chunk, dtype=F32)
    diff = idx[:, None] - idx[None, :]
    dmat = jnp.where(diff >= 0, jnp.exp(lg[:, None, None] * jnp.maximum(diff, 0.0)), 0.0)
    qdec = jnp.exp(lg[:, None] * (idx + 1.0))
    kdec = jnp.exp(lg[:, None] * (chunk - 1.0 - idx))
    pair_table = lambda t: jnp.repeat(t.reshape(R_HEADS // 2, 2, chunk), R_KEY_DIM, axis=1).transpose(0, 2, 1)
    lg32 = np.log(np.float32(1.0) - np.exp2(np.float32(-5.0) - np.arange(R_HEADS, dtype=np.float32)))
    state_decay = tuple(float(np.exp(lg32[h] * np.float32(chunk))) for h in range(R_HEADS))
    const = lambda shape: pl.BlockSpec(shape, lambda b, c: (0,) * len(shape))
    ns = RET_SEQS_PER_STEP if batch % RET_SEQS_PER_STEP == 0 else 1
    per_seq = lambda width: pl.BlockSpec((ns, chunk, width), lambda b, c: (b, c, 0))
    y, r_fin = pl.pallas_call(
        functools.partial(_retention_prompt_kernel, state_decay=state_decay),
        grid=(batch // ns, nc),
        in_specs=[
            per_seq(R_QK_WIDTH), per_seq(R_QK_WIDTH), per_seq(R_V_WIDTH),
            const((R_HEADS, chunk, chunk)),
            const((R_HEADS // 2, chunk, 2 * R_KEY_DIM)),
            const((R_HEADS // 2, chunk, 2 * R_KEY_DIM)),
        ],
        out_specs=[
            per_seq(R_V_WIDTH),
            pl.BlockSpec((ns, R_HEADS, R_KEY_DIM, R_VAL_DIM), lambda b, c: (b, 0, 0, 0)),
        ],
        out_shape=[
            jax.ShapeDtypeStruct((batch, seq, R_V_WIDTH), BF16),
            jax.ShapeDtypeStruct((batch, R_HEADS, R_KEY_DIM, R_VAL_DIM), F32),
        ],
        scratch_shapes=[pltpu.VMEM((ns, R_HEADS // 2, 2 * R_KEY_DIM, R_VAL_DIM), F32)],
        compiler_params=_params("arbitrary", "arbitrary"),
        name="retention_prompt",
    )(qr.reshape(batch, seq, R_QK_WIDTH), kr.reshape(batch, seq, R_QK_WIDTH), vr.reshape(batch, seq, R_V_WIDTH),
      dmat, pair_table(qdec), pair_table(kdec))
    return y.reshape(batch * seq, R_V_WIDTH), r_fin


def _merge_kernel(x_ref, ya_ref, yr_ref, g_ref, wg_ref, wm_ref, b_ref, wpa_ref, wpr_ref, wo_ref, gp_ref, y_ref,
                  *, sub_tiles):
    rows = x_ref.shape[0] // sub_tiles
    for t in range(sub_tiles):
        sl = slice(t * rows, (t + 1) * rows)
        x = x_ref[sl, :]
        h = _rms_normed(x, g_ref[...]).astype(BF16)
        gates = jnp.dot(h, wg_ref[...], preferred_element_type=F32)
        ua = (ya_ref[sl, :].astype(F32) * jax.nn.silu(gates[:, :A_WIDTH])).astype(BF16)
        ur = (yr_ref[sl, :].astype(F32) * jax.nn.silu(gates[:, A_WIDTH:])).astype(BF16)
        pa = jnp.dot(ua, wpa_ref[...], preferred_element_type=F32)
        pr = jnp.dot(ur, wpr_ref[...], preferred_element_type=F32)
        mix = jnp.dot(h, wm_ref[...], preferred_element_type=F32) + b_ref[...]
        m = jax.nn.sigmoid(mix[:, :D_MODEL]) * pa + jax.nn.sigmoid(mix[:, D_MODEL:]) * pr
        o = jnp.dot(m.astype(BF16), wo_ref[...], preferred_element_type=F32)
        y_ref[sl, :] = x + _rms_normed(o, gp_ref[...])


def _merge(x2d, ya, yr, g_pre, w_g, w_m, b_merge, w_pa, w_pr, w_out, g_post, *, tm):
    t = x2d.shape[0]
    row = lambda width: pl.BlockSpec((tm, width), lambda i: (i, 0))
    const = lambda a: pl.BlockSpec(a.shape, lambda i: (0,) * a.ndim)
    return pl.pallas_call(
        functools.partial(_merge_kernel, sub_tiles=max(1, tm // MERGE_SUB_ROWS)),
        grid=(t // tm,),
        in_specs=[row(D_MODEL), row(A_WIDTH), row(R_V_WIDTH), const(g_pre), const(w_g), const(w_m),
                  const(b_merge), const(w_pa), const(w_pr), const(w_out), const(g_post)],
        out_specs=row(D_MODEL),
        out_shape=jax.ShapeDtypeStruct((t, D_MODEL), F32),
        compiler_params=_params("arbitrary"),
        name="gate_merge",
    )(x2d, ya, yr, g_pre, w_g, w_m, b_merge, w_pa, w_pr, w_out, g_post)


def _store_lane_repeated_column(row, col_ref):
    eye = lax.broadcasted_iota(I32, (LANES, LANES), 0) == lax.broadcasted_iota(I32, (LANES, LANES), 1)
    for j in range(row.shape[1] // LANES):
        wide = jnp.broadcast_to(row[:, j * LANES:(j + 1) * LANES], (LANES, LANES))
        col = jnp.sum(jnp.where(eye, wide, 0.0), axis=1, keepdims=True)
        col_ref[j * LANES:(j + 1) * LANES, :] = jnp.broadcast_to(col, (LANES, col_ref.shape[1]))


def _sample_scores_init(q_ref, qcol_ref, part_ref):
    part_ref[...] = jnp.zeros(part_ref.shape, F32)
    _store_lane_repeated_column(q_ref[0], qcol_ref)


def _sample_scores_fold(c, pages_ref, qcol_ref, part_ref):
    blocks_per_step = pages_ref.shape[0] // PAGES_PER_BLOCK
    groups = A_HEAD_DIM // SUBLANES
    folded = [[None] * A_HEADS for _ in range(blocks_per_step)]
    for h in range(A_HEADS):
        qh = qcol_ref[h * A_HEAD_DIM:(h + 1) * A_HEAD_DIM, :]
        for r in range(blocks_per_step):
            tot = pages_ref[r * PAGES_PER_BLOCK, h]
            for half in range(1, PAGES_PER_BLOCK):
                tot = tot + pages_ref[r * PAGES_PER_BLOCK + half, h]
            folded[r][h] = jnp.sum((tot * qh).reshape(groups, SUBLANES, PAGE_SIZE), axis=0)
    lane = lax.broadcasted_iota(I32, part_ref.shape, 1)
    part = part_ref[...]
    for r in range(blocks_per_step):
        col = jnp.sum(jnp.concatenate(folded[r], axis=0), axis=1, keepdims=True)
        part = jnp.where(lane == c * blocks_per_step + r, col, part)
    part_ref[...] = part


def _sample_scores_emit(part_ref, score_ref, n_blocks):
    sc = jnp.sum(part_ref[...].reshape(A_HEADS, SUBLANES, LANES), axis=1) * (1.0 / MOBA_BLOCK)
    blk = lax.broadcasted_iota(I32, sc.shape, 1)
    score_ref[0] = jnp.where(blk < n_blocks, sc, -jnp.inf)


def _topk_kernel(score_ref, idx_ref):
    sc = score_ref[...]
    blk = lax.broadcasted_iota(I32, sc.shape, 1)
    out = jnp.zeros(sc.shape, I32)
    for t in range(MOBA_TOPK):
        mx = jnp.max(sc, axis=1, keepdims=True)
        first = jnp.min(jnp.where(sc == mx, blk, LANES), axis=1, keepdims=True)
        out = jnp.where(blk == t, first, out)
        sc = jnp.where(blk == first, -jnp.inf, sc)
    idx_ref[...] = out


def _select_topk(scores):
    rows = scores.shape[0] * scores.shape[1]
    return pl.pallas_call(
        _topk_kernel,
        out_shape=jax.ShapeDtypeStruct((rows, LANES), I32),
        compiler_params=pltpu.CompilerParams(vmem_limit_bytes=VMEM_LIMIT_BYTES),
        name="select_topk",
    )(scores.reshape(rows, LANES))


def _sample_attn_kernel(sel_ref, pt_ref, q_ref, ks_ref, vs_ref, ck_ref, cv_ref, o_ref, kbuf, vbuf, sems,
                        qcol_ref, ocol_ref, *, n_pages):
    b = pl.program_id(0)
    hd = A_HEAD_DIM
    slot = b % 2

    def slab_copies(page, head, dst_slot, j):
        return (pltpu.make_async_copy(ck_ref.at[page, head], kbuf.at[dst_slot, head, j], sems.at[dst_slot, 0]),
                pltpu.make_async_copy(cv_ref.at[page, head], vbuf.at[dst_slot, head, j], sems.at[dst_slot, 1]))

    def start_fetch(seq, dst_slot):
        for head in range(A_HEADS):
            for t in range(MOBA_TOPK):
                blk = sel_ref[(seq * A_HEADS + head) * MOBA_TOPK + t]
                for half in range(PAGES_PER_BLOCK):
                    page = pt_ref[seq * n_pages + blk * PAGES_PER_BLOCK + half]
                    for copy in slab_copies(page, head, dst_slot, t * PAGES_PER_BLOCK + half):
                        copy.start()

    @pl.when(b == 0)
    def _():
        start_fetch(0, 0)

    @pl.when(b + 1 < pl.num_programs(0))
    def _():
        start_fetch(b + 1, 1 - slot)

    for head in range(A_HEADS):
        for j in range(SAMPLE_SLABS):
            for copy in slab_copies(0, head, slot, j):
                copy.wait()

    groups = hd // SUBLANES
    q_row = q_ref[0] * (hd ** -0.5)
    _store_lane_repeated_column(q_row, qcol_ref)
    head_row = lax.broadcasted_iota(I32, (A_HEADS, PAGE_SIZE), 0)
    lane_head = lax.broadcasted_iota(I32, (1, A_WIDTH), 1) // hd

    def heads_to_lanes(col):
        row = jnp.zeros((1, A_WIDTH), F32)
        for head in range(A_HEADS):
            row = jnp.where(lane_head == head, col[head:head + 1, :], row)
        return row

    scores = []
    for j in range(SAMPLE_SLABS):
        folded = []
        for head in range(A_HEADS):
            prod = kbuf[slot, head, j] * qcol_ref[head * hd:(head + 1) * hd, :]
            folded.append(jnp.sum(prod.reshape(groups, SUBLANES, PAGE_SIZE), axis=0))
        stacked = jnp.concatenate(folded, axis=0).reshape(A_HEADS, SUBLANES, PAGE_SIZE)
        scores.append(jnp.sum(stacked, axis=1))
    qk_self = q_row * ks_ref[0]
    s_self = jnp.zeros((A_HEADS, 1), F32)
    for head in range(A_HEADS):
        part = jnp.sum(jnp.where(lane_head == head, qk_self, 0.0), axis=1, keepdims=True)
        s_self = jnp.where(head_row[:, :1] == head, part, s_self)

    m = scores[0]
    for sp in scores[1:]:
        m = jnp.maximum(m, sp)
    m = jnp.maximum(jnp.max(m, axis=1, keepdims=True), s_self)
    probs = [jnp.exp(sp - m) for sp in scores]
    p_self = jnp.exp(s_self - m)
    p_sum = probs[0]
    for p in probs[1:]:
        p_sum = p_sum + p
    l = jnp.sum(p_sum, axis=1, keepdims=True) + p_self

    for head in range(A_HEADS):
        acc = None
        for j in range(SAMPLE_SLABS):
            term = vbuf[slot, head, j] * probs[j][head:head + 1, :]
            acc = term if acc is None else acc + term
        ocol_ref[head * hd:(head + 1) * hd, :] = jnp.sum(acc, axis=1, keepdims=True)

    eye = lax.broadcasted_iota(I32, (LANES, LANES), 0) == lax.broadcasted_iota(I32, (LANES, LANES), 1)
    out_chunks = []
    for c in range(A_WIDTH // LANES):
        col = jnp.broadcast_to(ocol_ref[c * LANES:(c + 1) * LANES, :], (LANES, LANES))
        out_chunks.append(jnp.sum(jnp.where(eye, col, 0.0), axis=0, keepdims=True))
    out_row = jnp.concatenate(out_chunks, axis=1)
    o_ref[0] = (out_row + heads_to_lanes(p_self) * vs_ref[0]) / heads_to_lanes(l)


def _sample_attention(sel_flat, pt_flat, q_s, k_s, v_s, cache_kt, cache_vt, *, batch, n_pages):
    tok_spec = pl.BlockSpec((1, 1, A_WIDTH), lambda b, sel, pt: (b, 0, 0))
    slab_buf = pltpu.VMEM((2, A_HEADS, SAMPLE_SLABS, A_HEAD_DIM, PAGE_SIZE), F32)
    return pl.pallas_call(
        functools.partial(_sample_attn_kernel, n_pages=n_pages),
        grid_spec=pltpu.PrefetchScalarGridSpec(
            num_scalar_prefetch=2,
            grid=(batch,),
            in_specs=[tok_spec] * 3 + [pl.BlockSpec(memory_space=pl.ANY)] * 2,
            out_specs=tok_spec,
            scratch_shapes=[slab_buf, slab_buf, pltpu.SemaphoreType.DMA((2, 2)),
                            pltpu.VMEM((A_WIDTH, PAGE_SIZE), F32),
                            pltpu.VMEM((A_WIDTH, 1), F32)],
        ),
        out_shape=jax.ShapeDtypeStruct((batch, 1, A_WIDTH), F32),
        compiler_params=_params("arbitrary"),
        name="sample_attention",
    )(sel_flat, pt_flat, q_s.reshape(batch, 1, A_WIDTH), k_s.reshape(batch, 1, A_WIDTH),
      v_s.reshape(batch, 1, A_WIDTH), cache_kt, cache_vt)


def _sample_retention_kernel(q_ref, k_ref, v_ref, r_ref, y_ref, rnew_ref, *, decay):
    bb = q_ref.shape[0]
    kd, vd = R_KEY_DIM, R_VAL_DIM
    reps = LANES // bb
    qt = jnp.concatenate([q_ref[...]] * reps, axis=0).T
    kt = jnp.concatenate([k_ref[...]] * reps, axis=0).T
    for bi in range(bb):
        for h in range(R_HEADS):
            rows = slice(h * kd, (h + 1) * kd)
            qc = qt[rows, bi:bi + 1]
            kc = kt[rows, bi:bi + 1]
            vrow = v_ref[bi:bi + 1, h * vd:(h + 1) * vd]
            r = r_ref[bi, rows, :]
            qk = jnp.sum(qc * kc, axis=0, keepdims=True)
            cross = jnp.sum((qc * decay[h]) * r, axis=0, keepdims=True)
            o = qk * vrow + cross
            rnew_ref[bi, rows, :] = decay[h] * r + kc * vrow
            y = o * lax.rsqrt(jnp.mean(o * o, axis=-1, keepdims=True) + NORM_EPS)
            y_ref[bi:bi + 1, h * vd:(h + 1) * vd] = y


def _sample_retention(qr_s, kr_s, vr_s, state):
    batch = qr_s.shape[0]
    bb = SAMPLE_RET_BATCH
    lg32 = np.log(np.float32(1.0) - np.exp2(np.float32(-5.0) - np.arange(R_HEADS, dtype=np.float32)))
    decay = tuple(float(np.exp(lg32[h])) for h in range(R_HEADS))
    state3 = state.reshape(batch, R_HEADS * R_KEY_DIM, R_VAL_DIM)
    y, r_new = pl.pallas_call(
        functools.partial(_sample_retention_kernel, decay=decay),
        grid=(batch // bb,),
        in_specs=[
            pl.BlockSpec((bb, R_QK_WIDTH), lambda i: (i, 0)),
            pl.BlockSpec((bb, R_QK_WIDTH), lambda i: (i, 0)),
            pl.BlockSpec((bb, R_V_WIDTH), lambda i: (i, 0)),
            pl.BlockSpec((bb, R_HEADS * R_KEY_DIM, R_VAL_DIM), lambda i: (i, 0, 0)),
        ],
        out_specs=[
            pl.BlockSpec((bb, R_V_WIDTH), lambda i: (i, 0)),
            pl.BlockSpec((bb, R_HEADS * R_KEY_DIM, R_VAL_DIM), lambda i: (i, 0, 0)),
        ],
        out_shape=[
            jax.ShapeDtypeStruct((batch, R_V_WIDTH), F32),
            jax.ShapeDtypeStruct(state3.shape, F32),
        ],
        compiler_params=_params("arbitrary"),
        name="sample_retention",
    )(qr_s, kr_s, vr_s, state3)
    return y, r_new.reshape(state.shape)


def _layer(x_p, x_s, cache_k, cache_v, state_r, page_table, g_pre, w_in, b_merge, w_pa, w_pr, w_out, g_post):
    batch, seq, _ = x_p.shape
    dec_batch, dec_seq, _ = x_s.shape
    assert dec_seq == 1
    n_pages = page_table.shape[1]
    past_len = n_pages * PAGE_SIZE
    assert seq % PROMPT_TOKENS_PER_TILE == 0 and seq % RET_CHUNK == 0 and seq % (MOBA_GROUP * MOBA_BLOCK) == 0
    assert MOBA_TOPK <= n_pages // PAGES_PER_BLOCK <= LANES
    assert dec_batch % SAMPLE_RET_BATCH == 0

    o = _OFFS
    w_bf = w_in.astype(BF16)
    w_a = jnp.concatenate([w_bf[:, o[0]:o[3]], w_bf[:, o[4]:o[7]]], axis=1)
    w_g = jnp.concatenate([w_bf[:, o[3]:o[4]], w_bf[:, o[7]:o[8]]], axis=1)
    w_m = w_bf[:, o[8]:o[10]]
    w_pa_b, w_pr_b, w_out_b = w_pa.astype(BF16), w_pr.astype(BF16), w_out.astype(BF16)
    g_pre2, g_post2, b2 = g_pre.reshape(1, D_MODEL), g_post.reshape(1, D_MODEL), b_merge.reshape(1, 2 * D_MODEL)

    pos_p = jnp.arange(seq, dtype=I32)
    xp2 = x_p.reshape(batch * seq, D_MODEL)
    q, kf_t, kb, vf_t, vb_t, qr, kr, vr, km = _in_projection(
        xp2, g_pre2, w_a,
        _rope_tables(pos_p, ROPE_DIM, A_HEAD_DIM, ROPE_THETA),
        _rope_tables(pos_p, R_KEY_DIM, R_KEY_DIM, R_ROPE_THETA),
        tm=PROMPT_TOKENS_PER_TILE, prompt_seq=seq, act_dtype=BF16, q_scale=MOBA_Q_SCALE)
    km = km.reshape(batch * (seq // MOBA_BLOCK), A_WIDTH)
    pos_s = jnp.full((dec_batch,), past_len, dtype=I32)
    xs2 = x_s.reshape(dec_batch, D_MODEL)
    q_s, k_s, _, v_s, _, qr_s, kr_s, vr_s = _in_projection(
        xs2, g_pre2, w_a,
        _rope_tables(pos_s, ROPE_DIM, A_HEAD_DIM, ROPE_THETA),
        _rope_tables(pos_s, R_KEY_DIM, R_KEY_DIM, R_ROPE_THETA),
        tm=dec_batch, prompt_seq=None, act_dtype=F32, q_scale=None)
    cache_kt = cache_k.transpose(0, 2, 3, 1)
    cache_vt = cache_v.transpose(0, 2, 3, 1)
    pt_flat = page_table.reshape(-1)

    ya, block_scores = _moba_prompt_and_sample_select(q, kb, vb_t, km, q_s, cache_kt, pt_flat, batch=batch, seq=seq,
                                                      dec_batch=dec_batch, n_pages=n_pages)
    yr, r_p = _retention_prompt(qr, kr, vr, batch=batch, seq=seq)
    y_p = _merge(xp2, ya, yr, g_pre2, w_g, w_m, b2, w_pa_b, w_pr_b, w_out_b, g_post2, tm=PROMPT_TOKENS_PER_TILE)

    sel_flat = _select_topk(block_scores)[:, :MOBA_TOPK].reshape(-1)
    ya_s = _sample_attention(sel_flat, pt_flat, q_s, k_s, v_s, cache_kt, cache_vt,
                             batch=dec_batch, n_pages=n_pages).reshape(dec_batch, A_WIDTH)
    yr_s, r_s = _sample_retention(qr_s, kr_s, vr_s, state_r)
    y_s = _merge(xs2, ya_s, yr_s, g_pre2, w_g, w_m, b2, w_pa_b, w_pr_b, w_out_b, g_post2, tm=dec_batch)

    per_token = lambda a: a.reshape(batch, A_HEADS, A_HEAD_DIM, seq).transpose(0, 3, 1, 2)
    return (y_p.reshape(x_p.shape), y_s.reshape(x_s.shape), per_token(kf_t), per_token(vf_t), r_p,
            k_s.reshape(dec_batch, 1, A_HEADS, A_HEAD_DIM), v_s.reshape(dec_batch, 1, A_HEADS, A_HEAD_DIM), r_s)


def kernel(x_prompt, x_sample, cache_k, cache_v, state_ret, page_table, norm_pre_g, w_in, b_merge, w_proj_a,
           w_proj_r, w_out, norm_post_g):
    h_p, h_s = x_prompt, x_sample
    outs = [[] for _ in range(6)]
    for layer in range(w_in.shape[0]):
        res = _layer(h_p, h_s, cache_k[layer], cache_v[layer], state_ret[layer], page_table,
                     norm_pre_g[layer], w_in[layer], b_merge[layer], w_proj_a[layer], w_proj_r[layer],
                     w_out[layer], norm_post_g[layer])
        h_p, h_s = res[0], res[1]
        for acc, leaf in zip(outs, res[2:]):
            acc.append(leaf)
    return (h_p, h_s) + tuple(jnp.stack(o) for o in outs)
```

```python
import functools

import numpy as np
import jax
import jax.numpy as jnp
from jax import lax
from jax.experimental import pallas as pl
from jax.experimental.pallas import tpu as pltpu

F32 = jnp.float32
BF16 = jnp.bfloat16
I32 = jnp.int32

D_MODEL = 1024
A_HEADS = 8
A_HEAD_DIM = 64
A_WIDTH = A_HEADS * A_HEAD_DIM
MOBA_BLOCK = 256
MOBA_TOPK = 3
ROPE_THETA = 500000.0
ROPE_DIM = A_HEAD_DIM // 4
R_HEADS = 4
R_KEY_DIM = 64
R_VAL_DIM = 128
R_QK_WIDTH = R_HEADS * R_KEY_DIM
R_V_WIDTH = R_HEADS * R_VAL_DIM
R_ROPE_THETA = 10000.0
PAGE_SIZE = 128
NORM_EPS = 1e-6
NEG = -1e30

_OFFS = np.cumsum((0, A_WIDTH, A_WIDTH, A_WIDTH, A_WIDTH, R_QK_WIDTH, R_QK_WIDTH, R_V_WIDTH, R_V_WIDTH, D_MODEL, D_MODEL))

LANES = 128
SUBLANES = 8
VMEM_LIMIT_BYTES = 56 * 1024 * 1024

MOBA_Q_SCALE = A_HEAD_DIM ** -0.5 * float(np.log2(np.e))
MOBA_GROUP = 4
BF16_SUBLANES = 16
MOBA_V_ROWS = A_HEAD_DIM + BF16_SUBLANES
RET_CHUNK = 256
RET_SEQS_PER_STEP = 8
PROMPT_TOKENS_PER_TILE = 512
MERGE_SUB_ROWS = 256
PAGES_PER_BLOCK = MOBA_BLOCK // PAGE_SIZE
SAMPLE_SLABS = MOBA_TOPK * PAGES_PER_BLOCK
SAMPLE_RET_BATCH = 16


def _params(*semantics):
    return pltpu.CompilerParams(dimension_semantics=semantics, vmem_limit_bytes=VMEM_LIMIT_BYTES)


def _rms_normed(x, g):
    return x * lax.rsqrt(jnp.mean(x * x, axis=-1, keepdims=True) + NORM_EPS) * g


def _rope_tables(pos, rot_dim, head_dim, theta):
    half = rot_dim // 2
    inv_freq = theta ** (-jnp.arange(half, dtype=F32) * (2.0 / rot_dim))
    d = jnp.arange(LANES) % head_dim
    ang = pos.astype(F32)[:, None] * inv_freq[d % half][None, :]
    cos, sin = jnp.cos(ang), jnp.sin(ang)
    c = jnp.where(d < rot_dim, cos, 1.0)
    s = jnp.where(d < half, -sin, jnp.where(d < rot_dim, sin, 0.0))
    return c, s


def _retention_log_decay():
    return jnp.log(1.0 - jnp.exp2(-5.0 - jnp.arange(R_HEADS, dtype=F32)))


def _rope_chunks(z, rs, c_ref, s_ref, half, head_dim):
    c, s = c_ref[rs, :], s_ref[rs, :]
    first_half = lax.broadcasted_iota(I32, c.shape, 1) % head_dim < half
    out = []
    for j in range(z.shape[1] // LANES):
        zs = z[:, j * LANES:(j + 1) * LANES]
        up = pltpu.roll(zs, LANES - half, 1)
        down = pltpu.roll(zs, half, 1)
        out.append(zs * c + jnp.where(first_half, up, down) * s)
    return out


def _store_chunks(ref, rs, chunks, scale=None):
    for j, ch in enumerate(chunks):
        if scale is not None:
            ch = ch * scale
        ref[rs, j * LANES:(j + 1) * LANES] = ch.astype(ref.dtype)


def _store_chunks_token_minor(refs, rs, chunks):
    for j, ch in enumerate(chunks):
        ct = ch.T
        for ref in refs:
            ref[0, j * LANES:(j + 1) * LANES, rs] = ct.astype(ref.dtype)


def _inproj_kernel(x_ref, g_ref, w_ref, ca_ref, sa_ref, cr_ref, sr_ref,
                   q_ref, kf_ref, kb_ref, vf_ref, vb_ref, qr_ref, kr_ref, vr_ref, *km_refs, q_scale, token_minor_kv):
    a, rq, rv = A_WIDTH, R_QK_WIDTH, R_V_WIDTH
    rs = slice(0, x_ref.shape[0])
    h = _rms_normed(x_ref[...], g_ref[...]).astype(BF16)

    def proj(lo, hi):
        return jnp.dot(h, w_ref[:, lo:hi], preferred_element_type=F32)

    def rope_a(z):
        return _rope_chunks(z, rs, ca_ref, sa_ref, ROPE_DIM // 2, A_HEAD_DIM)

    def rope_r(z):
        return _rope_chunks(z, rs, cr_ref, sr_ref, R_KEY_DIM // 2, R_KEY_DIM)

    _store_chunks(q_ref, rs, rope_a(proj(0, a)), scale=q_scale)
    k_chunks = rope_a(proj(a, 2 * a))
    if token_minor_kv:
        _store_chunks_token_minor([kf_ref], rs, k_chunks)
    else:
        _store_chunks(kf_ref, rs, k_chunks)
    _store_chunks(kb_ref, rs, k_chunks)
    if km_refs:
        (km_ref,) = km_refs
        for j, ch in enumerate(k_chunks):
            for blk in range(ch.shape[0] // MOBA_BLOCK):
                part = ch[blk * MOBA_BLOCK:(blk + 1) * MOBA_BLOCK, :]
                km_ref[0, blk:blk + 1, j * LANES:(j + 1) * LANES] = (
                    jnp.sum(part, axis=0, keepdims=True) * (1.0 / MOBA_BLOCK))
    v = proj(2 * a, 3 * a)
    if token_minor_kv:
        _store_chunks_token_minor([vf_ref, vb_ref], rs, [v[:, j * LANES:(j + 1) * LANES] for j in range(a // LANES)])
    else:
        vf_ref[...] = v
        vb_ref[...] = v.astype(vb_ref.dtype)
    o = 3 * a
    _store_chunks(qr_ref, rs, rope_r(proj(o, o + rq)))
    _store_chunks(kr_ref, rs, rope_r(proj(o + rq, o + 2 * rq)), scale=R_KEY_DIM ** -0.5)
    vr_ref[...] = proj(o + 2 * rq, o + 2 * rq + rv).astype(vr_ref.dtype)


def _in_projection(x2d, g_pre, w_a, tabs_a, tabs_r, *, tm, prompt_seq, act_dtype, q_scale):
    t = x2d.shape[0]
    n = t // tm
    tab_blocks = tabs_a[0].shape[0] // tm
    tab_spec = pl.BlockSpec((tm, LANES), lambda i: (i % tab_blocks, 0))
    row = lambda width: pl.BlockSpec((tm, width), lambda i: (i, 0))
    const = lambda shape: pl.BlockSpec(shape, lambda i: (0,) * len(shape))
    if prompt_seq is None:
        kv_shape, kv_spec = (t, A_WIDTH), row(A_WIDTH)
    else:
        seq_blocks = prompt_seq // tm
        kv_shape = (t // prompt_seq, A_WIDTH, prompt_seq)
        kv_spec = pl.BlockSpec((1, A_WIDTH, tm), lambda i: (i // seq_blocks, 0, i % seq_blocks))
    out_shape = [
        jax.ShapeDtypeStruct((t, A_WIDTH), act_dtype),
        jax.ShapeDtypeStruct(kv_shape, F32),
        jax.ShapeDtypeStruct((t, A_WIDTH), BF16),
        jax.ShapeDtypeStruct(kv_shape, F32),
        jax.ShapeDtypeStruct(kv_shape, BF16),
        jax.ShapeDtypeStruct((t, R_QK_WIDTH), act_dtype),
        jax.ShapeDtypeStruct((t, R_QK_WIDTH), act_dtype),
        jax.ShapeDtypeStruct((t, R_V_WIDTH), act_dtype),
    ]
    out_specs = [row(A_WIDTH), kv_spec, row(A_WIDTH), kv_spec, kv_spec] + [row(R_QK_WIDTH)] * 2 + [row(R_V_WIDTH)]
    if prompt_seq is not None:
        nb = tm // MOBA_BLOCK
        out_shape.append(jax.ShapeDtypeStruct((n, nb, A_WIDTH), F32))
        out_specs.append(pl.BlockSpec((1, nb, A_WIDTH), lambda i: (i, 0, 0)))
    return pl.pallas_call(
        functools.partial(_inproj_kernel, q_scale=q_scale, token_minor_kv=prompt_seq is not None),
        grid=(n,),
        in_specs=[row(D_MODEL), const((1, D_MODEL)), const(w_a.shape)] + [tab_spec] * 4,
        out_specs=out_specs,
        out_shape=out_shape,
        compiler_params=_params("arbitrary"),
        name="in_projection",
    )(x2d, g_pre, w_a, *tabs_a, *tabs_r)


def _moba_prompt_kernel(pt_ref, q_ref, k_ref, v_ref, km_ref, qsel_ref, ck_ref, o_ref, score_ref,
                        vt_ref, qs_ref, bias_ref, acc_ref, m_ref, qcol_ref, part_ref, page_buf, page_sem,
                        s0_ref, s1_ref, s2_ref, s3_ref, *, sel_steps, sel_blocks):
    i = pl.program_id(2)
    nb = MOBA_BLOCK
    hd = A_HEAD_DIM
    n_blocks = k_ref.shape[0] // nb
    n_sel_pages = page_buf.shape[1]

    step = (pl.program_id(0) * pl.num_programs(1) + pl.program_id(1)) * pl.num_programs(2) + i
    n_steps = pl.num_programs(0) * pl.num_programs(1) * pl.num_programs(2)
    slot = step % 2
    sel_c = step % sel_steps

    def page_copy(page, dst_slot, r):
        return pltpu.make_async_copy(ck_ref.at[page], page_buf.at[dst_slot, r], page_sem.at[dst_slot])

    def start_pages(s, dst_slot):
        for r in range(n_sel_pages):
            page_copy(pt_ref[s * n_sel_pages + r], dst_slot, r).start()

    @pl.when(step == 0)
    def _():
        start_pages(0, 0)

    @pl.when(step + 1 < n_steps)
    def _():
        start_pages(step + 1, 1 - slot)

    @pl.when(sel_c == 0)
    def _():
        _sample_scores_init(qsel_ref, qcol_ref, part_ref)

    @pl.when(i == 0)
    def _():
        ones_rows = (lax.broadcasted_iota(I32, (MOBA_V_ROWS - hd, nb), 0) == 0).astype(BF16)
        for c in range(n_blocks):
            for h in range(2):
                vt_ref[c, h, 0:hd, :] = v_ref[0, h * hd:(h + 1) * hd, c * nb:(c + 1) * nb]
                vt_ref[c, h, hd:MOBA_V_ROWS, :] = ones_rows

    qt = q_ref[...].astype(F32).T
    row = lax.broadcasted_iota(I32, qt.shape, 0)
    km = km_ref[...]
    km_hi = km.astype(BF16)
    km_lo = (km - km_hi.astype(F32)).astype(BF16)
    jrow = lax.broadcasted_iota(I32, (n_blocks, nb), 0)
    for h in range(2):
        qb = jnp.where((row >= hd) if h else (row < hd), qt, 0.0).astype(BF16)
        qs_ref[h] = qb
        sc = (jnp.dot(km_hi, qb, preferred_element_type=F32)
              + jnp.dot(km_lo, qb, preferred_element_type=F32))
        sc = jnp.where(jrow < i, sc, NEG)
        cnt = jnp.zeros(sc.shape, I32)
        for jp in range(n_blocks):
            r = sc[jp:jp + 1, :]
            beats = (r > sc) | ((r == sc) & (jp < jrow))
            cnt = cnt + beats.astype(I32)
        sel = (jrow < i) & (cnt < MOBA_TOPK)
        bias = jnp.where(sel, 0.0, NEG)
        for jp in range(n_blocks):
            bias_ref[jp + 1, h:h + 1, :] = bias[jp:jp + 1, :]
    bias_ref[0, 0:2, :] = jnp.zeros((2, nb), F32)
    m_ref[0:2, :] = jnp.full((2, nb), NEG, F32)
    acc_ref[...] = jnp.zeros(acc_ref.shape, F32)

    key_idx = lax.broadcasted_iota(I32, (nb, nb), 0)
    qry_idx = lax.broadcasted_iota(I32, (nb, nb), 1)

    def scores_into(s_ref, blk, causal=False):
        kj = k_ref[pl.ds(pl.multiple_of(blk * nb, nb), nb), :]
        for h in range(2):
            s = jnp.dot(kj, qs_ref[h], preferred_element_type=F32)
            s_ref[h] = jnp.where(key_idx <= qry_idx, s, NEG) if causal else s

    def attend(s_ref, t, v_blk):
        quarter = nb // 4
        for h in range(2):
            brow = bias_ref[t, h:h + 1, :]
            parts = [s_ref[h, r * quarter:(r + 1) * quarter, :] for r in range(4)]
            tall = jnp.maximum(jnp.maximum(parts[0], parts[1]), jnp.maximum(parts[2], parts[3]))
            m_old = m_ref[h:h + 1, :]
            m_new = jnp.maximum(m_old, jnp.max(tall, axis=0, keepdims=True) + brow)
            alpha = jnp.exp2(m_old - m_new)
            offset = jnp.where(brow < 0.0, -NEG, m_new)
            p = jnp.exp2(s_ref[h] - offset).astype(BF16)
            pv = jnp.dot(vt_ref[v_blk, h], p, preferred_element_type=F32)
            acc_ref[h] = alpha * acc_ref[h] + pv
            m_ref[h:h + 1, :] = m_new

    bufs = (s0_ref, s1_ref, s2_ref, s3_ref)

    for r in range(n_sel_pages):
        page_copy(0, slot, r).wait()
    _sample_scores_fold(sel_c, page_buf.at[slot], qcol_ref, part_ref)

    scores_into(bufs[0], i, causal=True)
    for r in range(1, MOBA_GROUP):
        scores_into(bufs[r], r - 1)

    attend(bufs[0], 0, i)
    for r in range(1, MOBA_GROUP):
        attend(bufs[r], r, r - 1)

    def group_body(g, carry):
        base = MOBA_GROUP * g - 1
        for r in range(MOBA_GROUP):
            scores_into(bufs[r], base + r)
        for r in range(MOBA_GROUP):
            attend(bufs[r], base + r + 1, base + r)
        return carry

    lax.fori_loop(1, i // MOBA_GROUP + 1, group_body, 0)

    outs = [acc_ref[h, 0:hd, :] / acc_ref[h, hd:hd + 1, :] for h in range(2)]
    o_ref[...] = jnp.concatenate(outs, axis=0).T.astype(o_ref.dtype)

    @pl.when(sel_c == sel_steps - 1)
    def _():
        _sample_scores_emit(part_ref, score_ref, sel_blocks)


def _moba_prompt_and_sample_select(q, k, v, km, q_s, cache_kt, pt_flat, *, batch, seq, dec_batch, n_pages):
    nb = MOBA_BLOCK
    n_blocks = seq // nb
    pairs = A_HEADS // 2
    pw = 2 * A_HEAD_DIM
    n_steps = batch * pairs * n_blocks
    pages_per_step = dec_batch * n_pages // n_steps
    assert pages_per_step * n_steps == dec_batch * n_pages, "sample key pages must split evenly over the grid"
    assert pages_per_step % PAGES_PER_BLOCK == 0 and n_pages % pages_per_step == 0
    sel_steps = n_pages // pages_per_step

    def step_of(b, p, i):
        return (b * pairs + p) * n_blocks + i

    return pl.pallas_call(
        functools.partial(_moba_prompt_kernel, sel_steps=sel_steps, sel_blocks=n_pages // PAGES_PER_BLOCK),
        grid_spec=pltpu.PrefetchScalarGridSpec(
            num_scalar_prefetch=1,
            grid=(batch, pairs, n_blocks),
            in_specs=[
                pl.BlockSpec((nb, pw), lambda b, p, i, pt: (b * n_blocks + i, p)),
                pl.BlockSpec((seq, pw), lambda b, p, i, pt: (b, p)),
                pl.BlockSpec((1, pw, seq), lambda b, p, i, pt: (b, p, 0)),
                pl.BlockSpec((n_blocks, pw), lambda b, p, i, pt: (b, p)),
                pl.BlockSpec((1, 1, A_WIDTH), lambda b, p, i, pt: (step_of(b, p, i) // sel_steps, 0, 0)),
                pl.BlockSpec(memory_space=pl.ANY),
            ],
            out_specs=[
                pl.BlockSpec((nb, pw), lambda b, p, i, pt: (b * n_blocks + i, p)),
                pl.BlockSpec((1, A_HEADS, LANES), lambda b, p, i, pt: (step_of(b, p, i) // sel_steps, 0, 0)),
            ],
            scratch_shapes=[
                pltpu.VMEM((n_blocks, 2, MOBA_V_ROWS, nb), BF16),
                pltpu.VMEM((2, pw, nb), BF16),
                pltpu.VMEM((n_blocks + 1, SUBLANES, nb), F32),
                pltpu.VMEM((2, MOBA_V_ROWS, nb), F32),
                pltpu.VMEM((SUBLANES, nb), F32),
                pltpu.VMEM((A_WIDTH, PAGE_SIZE), F32),
                pltpu.VMEM((A_HEADS * SUBLANES, LANES), F32),
                pltpu.VMEM((2, pages_per_step, A_HEADS, A_HEAD_DIM, PAGE_SIZE), F32),
                pltpu.SemaphoreType.DMA((2,)),
            ] + [pltpu.VMEM((2, nb, nb), F32)] * MOBA_GROUP,
        ),
        out_shape=[
            jax.ShapeDtypeStruct((batch * seq, A_WIDTH), BF16),
            jax.ShapeDtypeStruct((dec_batch, A_HEADS, LANES), F32),
        ],
        compiler_params=_params("arbitrary", "arbitrary", "arbitrary"),
        name="moba_prompt",
    )(pt_flat, q, k, v, km, q_s.reshape(dec_batch, 1, A_WIDTH), cache_kt)


def _retention_prompt_kernel(q_ref, k_ref, v_ref, dmat_ref, qdec_ref, kdec_ref, y_ref, rfin_ref, r_ref,
                             *, state_decay):
    c = pl.program_id(1)
    kd, vd = R_KEY_DIM, R_VAL_DIM

    @pl.when(c == 0)
    def _():
        r_ref[...] = jnp.zeros(r_ref.shape, F32)

    n_seqs, chunk = q_ref.shape[0], q_ref.shape[1]
    lane = lax.broadcasted_iota(I32, (chunk, 2 * kd), 1)
    for s in range(n_seqs):
        for p in range(R_HEADS // 2):
            q2 = q_ref[s, :, p * 2 * kd:(p + 1) * 2 * kd].astype(F32)
            k2 = k_ref[s, :, p * 2 * kd:(p + 1) * 2 * kd]
            qd2 = q2 * qdec_ref[p]
            kdt = (k2.astype(F32) * kdec_ref[p]).T.astype(BF16)
            r2 = r_ref[s, p]
            r2b = r2.astype(BF16)
            for hl in range(2):
                h = 2 * p + hl
                own = (lane >= kd) == bool(hl)
                qm = jnp.where(own, q2, 0.0).astype(BF16)
                qdm = jnp.where(own, qd2, 0.0).astype(BF16)
                sc = lax.dot_general(qm, k2, (((1,), (1,)), ((), ())), preferred_element_type=F32) * dmat_ref[h]
                vh = v_ref[s, :, h * vd:(h + 1) * vd]
                o = (jnp.dot(sc.astype(BF16), vh, preferred_element_type=F32)
                     + jnp.dot(qdm, r2b, preferred_element_type=F32))
                sl = slice(hl * kd, (hl + 1) * kd)
                u = jnp.dot(kdt[sl, :], vh, preferred_element_type=F32)
                r_ref[s, p, sl, :] = state_decay[h] * r2[sl, :] + u
                y = o * lax.rsqrt(jnp.mean(o * o, axis=-1, keepdims=True) + NORM_EPS)
                y_ref[s, :, h * vd:(h + 1) * vd] = y.astype(y_ref.dtype)

    @pl.when(c == pl.num_programs(1) - 1)
    def _():
        for s in range(n_seqs):
            for h in range(R_HEADS):
                rfin_ref[s, h] = r_ref[s, h // 2, (h % 2) * kd:(h % 2 + 1) * kd, :]


def _retention_prompt(qr, kr, vr, *, batch, seq):
    chunk = RET_CHUNK
    nc = seq // chunk
    lg = _retention_log_decay()
    idx = jnp.arange(chunk, dtype=F32)
    diff = idx[:, None] - idx[None, :]
    dmat = jnp.where(diff >= 0, jnp.exp(lg[:, None, None] * jnp.maximum(diff, 0.0)), 0.0)
    qdec = jnp.exp(lg[:, None] * (idx + 1.0))
    kdec = jnp.exp(lg[:, None] * (chunk - 1.0 - idx))
    pair_table = lambda t: jnp.repeat(t.reshape(R_HEADS // 2, 2, chunk), R_KEY_DIM, axis=1).transpose(0, 2, 1)
    lg32 = np.log(np.float32(1.0) - np.exp2(np.float32(-5.0) - np.arange(R_HEADS, dtype=np.float32)))
    state_decay = tuple(float(np.exp(lg32[h] * np.float32(chunk))) for h in range(R_HEADS))
    const = lambda shape: pl.BlockSpec(shape, lambda b, c: (0,) * len(shape))
    ns = RET_SEQS_PER_STEP if batch % RET_SEQS_PER_STEP == 0 else 1
    per_seq = lambda width: pl.BlockSpec((ns, chunk, width), lambda b, c: (b, c, 0))
    y, r_fin = pl.pallas_call(
        functools.partial(_retention_prompt_kernel, state_decay=state_decay),
        grid=(batch // ns, nc),
        in_specs=[
            per_seq(R_QK_WIDTH), per_seq(R_QK_WIDTH), per_seq(R_V_WIDTH),
            const((R_HEADS, chunk, chunk)),
            const((R_HEADS // 2, chunk, 2 * R_KEY_DIM)),
            const((R_HEADS // 2, chunk, 2 * R_KEY_DIM)),
        ],
        out_specs=[
            per_seq(R_V_WIDTH),
            pl.BlockSpec((ns, R_HEADS, R_KEY_DIM, R_VAL_DIM), lambda b, c: (b, 0, 0, 0)),
        ],
        out_shape=[
            jax.ShapeDtypeStruct((batch, seq, R_V_WIDTH), BF16),
            jax.ShapeDtypeStruct((batch, R_HEADS, R_KEY_DIM, R_VAL_DIM), F32),
        ],
        scratch_shapes=[pltpu.VMEM((ns, R_HEADS // 2, 2 * R_KEY_DIM, R_VAL_DIM), F32)],
        compiler_params=_params("arbitrary", "arbitrary"),
        name="retention_prompt",
    )(qr.reshape(batch, seq, R_QK_WIDTH), kr.reshape(batch, seq, R_QK_WIDTH), vr.reshape(batch, seq, R_V_WIDTH),
      dmat, pair_table(qdec), pair_table(kdec))
    return y.reshape(batch * seq, R_V_WIDTH), r_fin


def _merge_kernel(x_ref, ya_ref, yr_ref, g_ref, wg_ref, wm_ref, b_ref, wpa_ref, wpr_ref, wo_ref, gp_ref, y_ref,
                  *, sub_tiles):
    rows = x_ref.shape[0] // sub_tiles
    for t in range(sub_tiles):
        sl = slice(t * rows, (t + 1) * rows)
        x = x_ref[sl, :]
        h = _rms_normed(x, g_ref[...]).astype(BF16)
        gates = jnp.dot(h, wg_ref[...], preferred_element_type=F32)
        ua = (ya_ref[sl, :].astype(F32) * jax.nn.silu(gates[:, :A_WIDTH])).astype(BF16)
        ur = (yr_ref[sl, :].astype(F32) * jax.nn.silu(gates[:, A_WIDTH:])).astype(BF16)
        pa = jnp.dot(ua, wpa_ref[...], preferred_element_type=F32)
        pr = jnp.dot(ur, wpr_ref[...], preferred_element_type=F32)
        mix = jnp.dot(h, wm_ref[...], preferred_element_type=F32) + b_ref[...]
        m = jax.nn.sigmoid(mix[:, :D_MODEL]) * pa + jax.nn.sigmoid(mix[:, D_MODEL:]) * pr
        o = jnp.dot(m.astype(BF16), wo_ref[...], preferred_element_type=F32)
        y_ref[sl, :] = x + _rms_normed(o, gp_ref[...])


def _merge(x2d, ya, yr, g_pre, w_g, w_m, b_merge, w_pa, w_pr, w_out, g_post, *, tm):
    t = x2d.shape[0]
    row = lambda width: pl.BlockSpec((tm, width), lambda i: (i, 0))
    const = lambda a: pl.BlockSpec(a.shape, lambda i: (0,) * a.ndim)
    return pl.pallas_call(
        functools.partial(_merge_kernel, sub_tiles=max(1, tm // MERGE_SUB_ROWS)),
        grid=(t // tm,),
        in_specs=[row(D_MODEL), row(A_WIDTH), row(R_V_WIDTH), const(g_pre), const(w_g), const(w_m),
                  const(b_merge), const(w_pa), const(w_pr), const(w_out), const(g_post)],
        out_specs=row(D_MODEL),
        out_shape=jax.ShapeDtypeStruct((t, D_MODEL), F32),
        compiler_params=_params("arbitrary"),
        name="gate_merge",
    )(x2d, ya, yr, g_pre, w_g, w_m, b_merge, w_pa, w_pr, w_out, g_post)


def _store_lane_repeated_column(row, col_ref):
    eye = lax.broadcasted_iota(I32, (LANES, LANES), 0) == lax.broadcasted_iota(I32, (LANES, LANES), 1)
    for j in range(row.shape[1] // LANES):
        wide = jnp.broadcast_to(row[:, j * LANES:(j + 1) * LANES], (LANES, LANES))
        col = jnp.sum(jnp.where(eye, wide, 0.0), axis=1, keepdims=True)
        col_ref[j * LANES:(j + 1) * LANES, :] = jnp.broadcast_to(col, (LANES, col_ref.shape[1]))


def _sample_scores_init(q_ref, qcol_ref, part_ref):
    part_ref[...] = jnp.zeros(part_ref.shape, F32)
    _store_lane_repeated_column(q_ref[0], qcol_ref)


def _sample_scores_fold(c, pages_ref, qcol_ref, part_ref):
    blocks_per_step = pages_ref.shape[0] // PAGES_PER_BLOCK
    groups = A_HEAD_DIM // SUBLANES
    folded = [[None] * A_HEADS for _ in range(blocks_per_step)]
    for h in range(A_HEADS):
        qh = qcol_ref[h * A_HEAD_DIM:(h + 1) * A_HEAD_DIM, :]
        for r in range(blocks_per_step):
            tot = pages_ref[r * PAGES_PER_BLOCK, h]
            for half in range(1, PAGES_PER_BLOCK):
                tot = tot + pages_ref[r * PAGES_PER_BLOCK + half, h]
            folded[r][h] = jnp.sum((tot * qh).reshape(groups, SUBLANES, PAGE_SIZE), axis=0)
    lane = lax.broadcasted_iota(I32, part_ref.shape, 1)
    part = part_ref[...]
    for r in range(blocks_per_step):
        col = jnp.sum(jnp.concatenate(folded[r], axis=0), axis=1, keepdims=True)
        part = jnp.where(lane == c * blocks_per_step + r, col, part)
    part_ref[...] = part


def _sample_scores_emit(part_ref, score_ref, n_blocks):
    sc = jnp.sum(part_ref[...].reshape(A_HEADS, SUBLANES, LANES), axis=1) * (1.0 / MOBA_BLOCK)
    blk = lax.broadcasted_iota(I32, sc.shape, 1)
    score_ref[0] = jnp.where(blk < n_blocks, sc, -jnp.inf)


def _topk_kernel(score_ref, idx_ref):
    sc = score_ref[...]
    blk = lax.broadcasted_iota(I32, sc.shape, 1)
    out = jnp.zeros(sc.shape, I32)
    for t in range(MOBA_TOPK):
        mx = jnp.max(sc, axis=1, keepdims=True)
        first = jnp.min(jnp.where(sc == mx, blk, LANES), axis=1, keepdims=True)
        out = jnp.where(blk == t, first, out)
        sc = jnp.where(blk == first, -jnp.inf, sc)
    idx_ref[...] = out


def _select_topk(scores):
    rows = scores.shape[0] * scores.shape[1]
    return pl.pallas_call(
        _topk_kernel,
        out_shape=jax.ShapeDtypeStruct((rows, LANES), I32),
        compiler_params=pltpu.CompilerParams(vmem_limit_bytes=VMEM_LIMIT_BYTES),
        name="select_topk",
    )(scores.reshape(rows, LANES))


def _sample_attn_kernel(sel_ref, pt_ref, q_ref, ks_ref, vs_ref, ck_ref, cv_ref, o_ref, kbuf, vbuf, sems,
                        qcol_ref, ocol_ref, *, n_pages):
    b = pl.program_id(0)
    hd = A_HEAD_DIM
    slot = b % 2

    def slab_copies(page, head, dst_slot, j):
        return (pltpu.make_async_copy(ck_ref.at[page, head], kbuf.at[dst_slot, head, j], sems.at[dst_slot, 0]),
                pltpu.make_async_copy(cv_ref.at[page, head], vbuf.at[dst_slot, head, j], sems.at[dst_slot, 1]))

    def start_fetch(seq, dst_slot):
        for head in range(A_HEADS):
            for t in range(MOBA_TOPK):
                blk = sel_ref[(seq * A_HEADS + head) * MOBA_TOPK + t]
                for half in range(PAGES_PER_BLOCK):
                    page = pt_ref[seq * n_pages + blk * PAGES_PER_BLOCK + half]
                    for copy in slab_copies(page, head, dst_slot, t * PAGES_PER_BLOCK + half):
                        copy.start()

    @pl.when(b == 0)
    def _():
        start_fetch(0, 0)

    @pl.when(b + 1 < pl.num_programs(0))
    def _():
        start_fetch(b + 1, 1 - slot)

    for head in range(A_HEADS):
        for j in range(SAMPLE_SLABS):
            for copy in slab_copies(0, head, slot, j):
                copy.wait()

    groups = hd // SUBLANES
    q_row = q_ref[0] * (hd ** -0.5)
    _store_lane_repeated_column(q_row, qcol_ref)
    head_row = lax.broadcasted_iota(I32, (A_HEADS, PAGE_SIZE), 0)
    lane_head = lax.broadcasted_iota(I32, (1, A_WIDTH), 1) // hd

    def heads_to_lanes(col):
        row = jnp.zeros((1, A_WIDTH), F32)
        for head in range(A_HEADS):
            row = jnp.where(lane_head == head, col[head:head + 1, :], row)
        return row

    scores = []
    for j in range(SAMPLE_SLABS):
        folded = []
        for head in range(A_HEADS):
            prod = kbuf[slot, head, j] * qcol_ref[head * hd:(head + 1) * hd, :]
            folded.append(jnp.sum(prod.reshape(groups, SUBLANES, PAGE_SIZE), axis=0))
        stacked = jnp.concatenate(folded, axis=0).reshape(A_HEADS, SUBLANES, PAGE_SIZE)
        scores.append(jnp.sum(stacked, axis=1))
    qk_self = q_row * ks_ref[0]
    s_self = jnp.zeros((A_HEADS, 1), F32)
    for head in range(A_HEADS):
        part = jnp.sum(jnp.where(lane_head == head, qk_self, 0.0), axis=1, keepdims=True)
        s_self = jnp.where(head_row[:, :1] == head, part, s_self)

    m = scores[0]
    for sp in scores[1:]:
        m = jnp.maximum(m, sp)
    m = jnp.maximum(jnp.max(m, axis=1, keepdims=True), s_self)
    probs = [jnp.exp(sp - m) for sp in scores]
    p_self = jnp.exp(s_self - m)
    p_sum = probs[0]
    for p in probs[1:]:
        p_sum = p_sum + p
    l = jnp.sum(p_sum, axis=1, keepdims=True) + p_self

    for head in range(A_HEADS):
        acc = None
        for j in range(SAMPLE_SLABS):
            term = vbuf[slot, head, j] * probs[j][head:head + 1, :]
            acc = term if acc is None else acc + term
        ocol_ref[head * hd:(head + 1) * hd, :] = jnp.sum(acc, axis=1, keepdims=True)

    eye = lax.broadcasted_iota(I32, (LANES, LANES), 0) == lax.broadcasted_iota(I32, (LANES, LANES), 1)
    out_chunks = []
    for c in range(A_WIDTH // LANES):
        col = jnp.broadcast_to(ocol_ref[c * LANES:(c + 1) * LANES, :], (LANES, LANES))
        out_chunks.append(jnp.sum(jnp.where(eye, col, 0.0), axis=0, keepdims=True))
    out_row = jnp.concatenate(out_chunks, axis=1)
    o_ref[0] = (out_row + heads_to_lanes(p_self) * vs_ref[0]) / heads_to_lanes(l)


def _sample_attention(sel_flat, pt_flat, q_s, k_s, v_s, cache_kt, cache_vt, *, batch, n_pages):
    tok_spec = pl.BlockSpec((1, 1, A_WIDTH), lambda b, sel, pt: (b, 0, 0))
    slab_buf = pltpu.VMEM((2, A_HEADS, SAMPLE_SLABS, A_HEAD_DIM, PAGE_SIZE), F32)
    return pl.pallas_call(
        functools.partial(_sample_attn_kernel, n_pages=n_pages),
        grid_spec=pltpu.PrefetchScalarGridSpec(
            num_scalar_prefetch=2,
            grid=(batch,),
            in_specs=[tok_spec] * 3 + [pl.BlockSpec(memory_space=pl.ANY)] * 2,
            out_specs=tok_spec,
            scratch_shapes=[slab_buf, slab_buf, pltpu.SemaphoreType.DMA((2, 2)),
                            pltpu.VMEM((A_WIDTH, PAGE_SIZE), F32),
                            pltpu.VMEM((A_WIDTH, 1), F32)],
        ),
        out_shape=jax.ShapeDtypeStruct((batch, 1, A_WIDTH), F32),
        compiler_params=_params("arbitrary"),
        name="sample_attention",
    )(sel_flat, pt_flat, q_s.reshape(batch, 1, A_WIDTH), k_s.reshape(batch, 1, A_WIDTH),
      v_s.reshape(batch, 1, A_WIDTH), cache_kt, cache_vt)


def _sample_retention_kernel(q_ref, k_ref, v_ref, r_ref, y_ref, rnew_ref, *, decay):
    bb = q_ref.shape[0]
    kd, vd = R_KEY_DIM, R_VAL_DIM
    reps = LANES // bb
    qt = jnp.concatenate([q_ref[...]] * reps, axis=0).T
    kt = jnp.concatenate([k_ref[...]] * reps, axis=0).T
    for bi in range(bb):
        for h in range(R_HEADS):
            rows = slice(h * kd, (h + 1) * kd)
            qc = qt[rows, bi:bi + 1]
            kc = kt[rows, bi:bi + 1]
            vrow = v_ref[bi:bi + 1, h * vd:(h + 1) * vd]
            r = r_ref[bi, rows, :]
            qk = jnp.sum(qc * kc, axis=0, keepdims=True)
            cross = jnp.sum((qc * decay[h]) * r, axis=0, keepdims=True)
            o = qk * vrow + cross
            rnew_ref[bi, rows, :] = decay[h] * r + kc * vrow
            y = o * lax.rsqrt(jnp.mean(o * o, axis=-1, keepdims=True) + NORM_EPS)
            y_ref[bi:bi + 1, h * vd:(h + 1) * vd] = y


def _sample_retention(qr_s, kr_s, vr_s, state):
    batch = qr_s.shape[0]
    bb = SAMPLE_RET_BATCH
    lg32 = np.log(np.float32(1.0) - np.exp2(np.float32(-5.0) - np.arange(R_HEADS, dtype=np.float32)))
    decay = tuple(float(np.exp(lg32[h])) for h in range(R_HEADS))
    state3 = state.reshape(batch, R_HEADS * R_KEY_DIM, R_VAL_DIM)
    y, r_new = pl.pallas_call(
        functools.partial(_sample_retention_kernel, decay=decay),
        grid=(batch // bb,),
        in_specs=[
            pl.BlockSpec((bb, R_QK_WIDTH), lambda i: (i, 0)),
            pl.BlockSpec((bb, R_QK_WIDTH), lambda i: (i, 0)),
            pl.BlockSpec((bb, R_V_WIDTH), lambda i: (i, 0)),
            pl.BlockSpec((bb, R_HEADS * R_KEY_DIM, R_VAL_DIM), lambda i: (i, 0, 0)),
        ],
        out_specs=[
            pl.BlockSpec((bb, R_V_WIDTH), lambda i: (i, 0)),
            pl.BlockSpec((bb, R_HEADS * R_KEY_DIM, R_VAL_DIM), lambda i: (i, 0, 0)),
        ],
        out_shape=[
            jax.ShapeDtypeStruct((batch, R_V_WIDTH), F32),
            jax.ShapeDtypeStruct(state3.shape, F32),
        ],
        compiler_params=_params("arbitrary"),
        name="sample_retention",
    )(qr_s, kr_s, vr_s, state3)
    return y, r_new.reshape(state.shape)


def _layer(x_p, x_s, cache_k, cache_v, state_r, page_table, g_pre, w_in, b_merge, w_pa, w_pr, w_out, g_post):
    batch, seq, _ = x_p.shape
    dec_batch, dec_seq, _ = x_s.shape
    assert dec_seq == 1
    n_pages = page_table.shape[1]
    past_len = n_pages * PAGE_SIZE
    assert seq % PROMPT_TOKENS_PER_TILE == 0 and seq % RET_CHUNK == 0 and seq % (MOBA_GROUP * MOBA_BLOCK) == 0
    assert MOBA_TOPK <= n_pages // PAGES_PER_BLOCK <= LANES
    assert dec_batch % SAMPLE_RET_BATCH == 0

    o = _OFFS
    w_bf = w_in.astype(BF16)
    w_a = jnp.concatenate([w_bf[:, o[0]:o[3]], w_bf[:, o[4]:o[7]]], axis=1)
    w_g = jnp.concatenate([w_bf[:, o[3]:o[4]], w_bf[:, o[7]:o[8]]], axis=1)
    w_m = w_bf[:, o[8]:o[10]]
    w_pa_b, w_pr_b, w_out_b = w_pa.astype(BF16), w_pr.astype(BF16), w_out.astype(BF16)
    g_pre2, g_post2, b2 = g_pre.reshape(1, D_MODEL), g_post.reshape(1, D_MODEL), b_merge.reshape(1, 2 * D_MODEL)

    pos_p = jnp.arange(seq, dtype=I32)
    xp2 = x_p.reshape(batch * seq, D_MODEL)
    q, kf_t, kb, vf_t, vb_t, qr, kr, vr, km = _in_projection(
        xp2, g_pre2, w_a,
        _rope_tables(pos_p, ROPE_DIM, A_HEAD_DIM, ROPE_THETA),
        _rope_tables(pos_p, R_KEY_DIM, R_KEY_DIM, R_ROPE_THETA),
        tm=PROMPT_TOKENS_PER_TILE, prompt_seq=seq, act_dtype=BF16, q_scale=MOBA_Q_SCALE)
    km = km.reshape(batch * (seq // MOBA_BLOCK), A_WIDTH)
    pos_s = jnp.full((dec_batch,), past_len, dtype=I32)
    xs2 = x_s.reshape(dec_batch, D_MODEL)
    q_s, k_s, _, v_s, _, qr_s, kr_s, vr_s = _in_projection(
        xs2, g_pre2, w_a,
        _rope_tables(pos_s, ROPE_DIM, A_HEAD_DIM, ROPE_THETA),
        _rope_tables(pos_s, R_KEY_DIM, R_KEY_DIM, R_ROPE_THETA),
        tm=dec_batch, prompt_seq=None, act_dtype=F32, q_scale=None)
    cache_kt = cache_k.transpose(0, 2, 3, 1)
    cache_vt = cache_v.transpose(0, 2, 3, 1)
    pt_flat = page_table.reshape(-1)

    ya, block_scores = _moba_prompt_and_sample_select(q, kb, vb_t, km, q_s, cache_kt, pt_flat, batch=batch, seq=seq,
                                                      dec_batch=dec_batch, n_pages=n_pages)
    yr, r_p = _retention_prompt(qr, kr, vr, batch=batch, seq=seq)
    y_p = _merge(xp2, ya, yr, g_pre2, w_g, w_m, b2, w_pa_b, w_pr_b, w_out_b, g_post2, tm=PROMPT_TOKENS_PER_TILE)

    sel_flat = _select_topk(block_scores)[:, :MOBA_TOPK].reshape(-1)
    ya_s = _sample_attention(sel_flat, pt_flat, q_s, k_s, v_s, cache_kt, cache_vt,
                             batch=dec_batch, n_pages=n_pages).reshape(dec_batch, A_WIDTH)
    yr_s, r_s = _sample_retention(qr_s, kr_s, vr_s, state_r)
    y_s = _merge(xs2, ya_s, yr_s, g_pre2, w_g, w_m, b2, w_pa_b, w_pr_b, w_out_b, g_post2, tm=dec_batch)

    per_token = lambda a: a.reshape(batch, A_HEADS, A_HEAD_DIM, seq).transpose(0, 3, 1, 2)
    return (y_p.reshape(x_p.shape), y_s.reshape(x_s.shape), per_token(kf_t), per_token(vf_t), r_p,
            k_s.reshape(dec_batch, 1, A_HEADS, A_HEAD_DIM), v_s.reshape(dec_batch, 1, A_HEADS, A_HEAD_DIM), r_s)


def kernel(x_prompt, x_sample, cache_k, cache_v, state_ret, page_table, norm_pre_g, w_in, b_merge, w_proj_a,
           w_proj_r, w_out, norm_post_g):
    h_p, h_s = x_prompt, x_sample
    outs = [[] for _ in range(6)]
    for layer in range(w_in.shape[0]):
        res = _layer(h_p, h_s, cache_k[layer], cache_v[layer], state_ret[layer], page_table,
                     norm_pre_g[layer], w_in[layer], b_merge[layer], w_proj_a[layer], w_proj_r[layer],
                     w_out[layer], norm_post_g[layer])
        h_p, h_s = res[0], res[1]
        for acc, leaf in zip(outs, res[2:]):
            acc.append(leaf)
    return (h_p, h_s) + tuple(jnp.stack(o) for o in outs)
```
